```python
import math
import jax, jax.numpy as jnp
from jax import lax
import numpy as np

D_MODEL = 1024
BATCH = 4
SEQ = 8192
DEPTH = 1

MLA_HEADS = 8
MLA_Q_RANK = 384
MLA_KV_RANK = 256
MLA_NOPE_DIM = 64
MLA_ROPE_DIM = 32
MLA_V_DIM = 64
MLA_QK_DIM = MLA_NOPE_DIM + MLA_ROPE_DIM
FOX_HEADS = 8
FOX_HEAD_DIM = 64
FOX_WIDTH = FOX_HEADS * FOX_HEAD_DIM
MIX_WIDTH = MLA_HEADS * MLA_V_DIM + FOX_WIDTH
ROPE_THETA = 10000.0
BLOCK_Q = 128
IN_SPLITS = (MLA_Q_RANK, MLA_KV_RANK, MLA_ROPE_DIM, FOX_WIDTH, FOX_WIDTH, FOX_WIDTH, FOX_HEADS)
IN_WIDTH = sum(IN_SPLITS)
PEER_HEADS = 8
N_KEYS = 128
N_EXPERTS = N_KEYS * N_KEYS
PEER_QDIM = 256
PEER_HALF = PEER_QDIM // 2
PEER_TOPK = 16
PEER_CHUNK = 64
EPS = 1e-6

kernel_name = "hybrid_mla_fox_peer_adaln_layer"


def _rms_norm(x, g):
    xf = x.astype(jnp.float32)
    y = xf * lax.rsqrt(jnp.mean(xf * xf, axis=-1, keepdims=True) + EPS)
    return (y * g.astype(jnp.float32)).astype(x.dtype)


def _modulate(x, g, shift, scale):
    return _rms_norm(x, g) * (1.0 + scale) + shift


def _rope(x, ang):
    half = x.shape[-1] // 2
    x1, x2 = x[..., :half], x[..., half:]
    cos = jnp.cos(ang).astype(x.dtype)
    sin = jnp.sin(ang).astype(x.dtype)
    return jnp.concatenate([x1 * cos - x2 * sin, x2 * cos + x1 * sin], axis=-1)


def _causal_block_attention(q, k, v, scale, fcum=None):
    b, h, s, dk = q.shape
    dv = v.shape[-1]
    nblk = s // BLOCK_Q
    qb = q.reshape(b, h, nblk, BLOCK_Q, dk).transpose(2, 0, 1, 3, 4)
    kf = k.astype(jnp.float32)
    vf = v.astype(jnp.float32)
    key_pos = jnp.arange(s)
    blk_ids = jnp.arange(nblk)

    def scores(q_i):
        return jnp.einsum("bhqd,bhkd->bhqk", q_i.astype(jnp.float32), kf) * scale

    def finish(i, sc):
        q_pos = i * BLOCK_Q + jnp.arange(BLOCK_Q)
        sc = jnp.where(key_pos[None, :] <= q_pos[:, None], sc, -jnp.inf)
        p = jax.nn.softmax(sc, axis=-1)
        return jnp.einsum("bhqk,bhkd->bhqd", p, vf)

    if fcum is None:
        out = lax.map(lambda a: finish(a[0], scores(a[1])), (blk_ids, qb))
    else:
        fb = fcum.reshape(b, h, nblk, BLOCK_Q).transpose(2, 0, 1, 3)

        def blk(a):
            i, q_i, f_i = a
            sc = scores(q_i) + f_i[..., None] - fcum[:, :, None, :]
            return finish(i, sc)

        out = lax.map(blk, (blk_ids, qb, fb))
    return out.transpose(1, 2, 0, 3, 4).reshape(b, h, s, dv).astype(v.dtype)


def _peer(h, w_pq, sub_keys, peer_u, peer_v):
    b, s, d = h.shape
    qp = (h @ w_pq).astype(jnp.float32).reshape(b, s, PEER_HEADS, 2, PEER_HALF)
    sc = jnp.einsum("bshpd,hpnd->bshpn", qp, sub_keys.astype(jnp.float32))
    s1, i1 = lax.top_k(sc[..., 0, :], PEER_TOPK)
    s2, i2 = lax.top_k(sc[..., 1, :], PEER_TOPK)
    cand = (s1[..., :, None] + s2[..., None, :]).reshape(b, s, PEER_HEADS, PEER_TOPK * PEER_TOPK)
    top_s, top_c = lax.top_k(cand, PEER_TOPK)
    e1 = jnp.take_along_axis(i1, top_c // PEER_TOPK, axis=-1)
    e2 = jnp.take_along_axis(i2, top_c % PEER_TOPK, axis=-1)
    expert = e1 * N_KEYS + e2
    gate = jax.nn.softmax(top_s, axis=-1)
    nc = s // PEER_CHUNK

    def to_chunks(a):
        return a.reshape((b, nc, PEER_CHUNK) + a.shape[2:]).swapaxes(0, 1)

    def chunk(args):
        h_c, e_c, g_c = args
        u = jnp.take(peer_u, e_c, axis=0)
        a = jnp.einsum("bcd,bchkd->bchk", h_c.astype(jnp.float32), u.astype(jnp.float32))
        w = g_c * jax.nn.gelu(a)
        vv = jnp.take(peer_v, e_c, axis=0)
        return jnp.einsum("bchk,bchkd->bcd", w, vv.astype(jnp.float32))

    y = lax.map(chunk, (to_chunks(h), to_chunks(expert), to_chunks(gate)))
    return y.swapaxes(0, 1).reshape(b, s, d).astype(h.dtype)


def setup_inputs(seed: int = 0) -> dict:
    key = jax.random.key(seed)
    ks = jax.random.split(key, 24)
    f32 = jnp.float32
    D = D_MODEL

    def nrm(k, shape, scale):
        return jax.random.normal(k, shape, f32) * scale

    def gain(k, n):
        return 1.0 + 0.02 * jax.random.normal(k, (n,), f32)

    x = jax.random.normal(ks[0], (BATCH, SEQ, D), f32)
    c = jax.random.normal(ks[1], (BATCH, D), f32)
    offset = jax.random.randint(ks[2], (BATCH, 1), 0, 1024, dtype=jnp.int32)
    positions = (offset + jnp.arange(SEQ, dtype=jnp.int32)[None, :]).astype(jnp.int32)
    return {
        "x": x,
        "c": c,
        "positions": positions,
        "w_ada": nrm(ks[3], (D, 6 * D), 0.5 * D ** -0.5),
        "b_ada": nrm(ks[4], (6 * D,), 0.01),
        "norm1_g": gain(ks[5], D),
        "w_in": nrm(ks[6], (D, IN_WIDTH), D ** -0.5),
        "mla_qa_g": gain(ks[7], MLA_Q_RANK),
        "mla_kva_g": gain(ks[8], MLA_KV_RANK),
        "w_uq": nrm(ks[9], (MLA_Q_RANK, MLA_HEADS * MLA_QK_DIM), MLA_Q_RANK ** -0.5),
        "w_ukv": nrm(ks[10], (MLA_KV_RANK, MLA_HEADS * (MLA_NOPE_DIM + MLA_V_DIM)), MLA_KV_RANK ** -0.5),
        "mla_q_g": gain(ks[11], MLA_QK_DIM),
        "mla_k_g": gain(ks[12], MLA_QK_DIM),
        "fox_q_g": gain(ks[13], FOX_HEAD_DIM),
        "fox_k_g": gain(ks[14], FOX_HEAD_DIM),
        "b_f": 2.0 + 0.5 * jax.random.normal(ks[15], (FOX_HEADS,), f32),
        "w_o": nrm(ks[16], (MIX_WIDTH, D), MIX_WIDTH ** -0.5),
        "norm2_g": gain(ks[17], D),
        "w_pq": nrm(ks[18], (D, PEER_HEADS * PEER_QDIM), D ** -0.5),
        "sub_keys": nrm(ks[19], (PEER_HEADS, 2, N_KEYS, PEER_HALF), PEER_HALF ** -0.5),
        "peer_u": nrm(ks[20], (N_EXPERTS, D), D ** -0.5),
        "peer_v": nrm(ks[21], (N_EXPERTS, D), PEER_HEADS ** -0.5),
    }


def reference(x, c, positions, w_ada, b_ada, norm1_g, w_in, mla_qa_g, mla_kva_g, w_uq, w_ukv,
              mla_q_g, mla_k_g, fox_q_g, fox_k_g, b_f, w_o, norm2_g, w_pq, sub_keys, peer_u, peer_v):
    dt = x.dtype
    b, s, d = x.shape
    mod = jax.nn.silu(c.astype(jnp.float32)) @ w_ada.astype(jnp.float32) + b_ada.astype(jnp.float32)
    mod = mod.reshape(b, 6, 1, d).astype(dt)
    shift1, scale1, gate1 = mod[:, 0], mod[:, 1], mod[:, 2]
    shift2, scale2, gate2 = mod[:, 3], mod[:, 4], mod[:, 5]
    inv_freq = ROPE_THETA ** (-jnp.arange(0, MLA_ROPE_DIM, 2, dtype=jnp.float32) / MLA_ROPE_DIM)
    ang = (positions.astype(jnp.float32)[..., None] * inv_freq)[:, :, None, :]

    for _ in range(DEPTH):
        h = _modulate(x, norm1_g, shift1, scale1)
        proj = h @ w_in
        cq, ckv, kpe, fq, fk, fv, fl = jnp.split(proj, list(np.cumsum(IN_SPLITS)[:-1]), axis=-1)

        q_a = (_rms_norm(cq, mla_qa_g) @ w_uq).reshape(b, s, MLA_HEADS, MLA_QK_DIM)
        kv_a = (_rms_norm(ckv, mla_kva_g) @ w_ukv).reshape(b, s, MLA_HEADS, MLA_NOPE_DIM + MLA_V_DIM)
        k_nope, v_a = kv_a[..., :MLA_NOPE_DIM], kv_a[..., MLA_NOPE_DIM:]
        k_pe = jnp.broadcast_to(kpe[:, :, None, :], (b, s, MLA_HEADS, MLA_ROPE_DIM))
        k_a = jnp.concatenate([k_nope, k_pe], axis=-1)
        q_a = _rms_norm(q_a, mla_q_g)
        k_a = _rms_norm(k_a, mla_k_g)
        q_a = jnp.concatenate([q_a[..., :MLA_NOPE_DIM], _rope(q_a[..., MLA_NOPE_DIM:], ang)], axis=-1)
        k_a = jnp.concatenate([k_a[..., :MLA_NOPE_DIM], _rope(k_a[..., MLA_NOPE_DIM:], ang)], axis=-1)
        o_a = _causal_block_attention(q_a.transpose(0, 2, 1, 3), k_a.transpose(0, 2, 1, 3),
                                      v_a.transpose(0, 2, 1, 3), MLA_QK_DIM ** -0.5)

        q_b = _rms_norm(fq.reshape(b, s, FOX_HEADS, FOX_HEAD_DIM), fox_q_g)
        k_b = _rms_norm(fk.reshape(b, s, FOX_HEADS, FOX_HEAD_DIM), fox_k_g)
        v_b = fv.reshape(b, s, FOX_HEADS, FOX_HEAD_DIM)
        log_f = jax.nn.log_sigmoid(fl.astype(jnp.float32) + b_f.astype(jnp.float32))
        fcum = jnp.cumsum(log_f, axis=1).transpose(0, 2, 1)
        o_b = _causal_block_attention(q_b.transpose(0, 2, 1, 3), k_b.transpose(0, 2, 1, 3),
                                      v_b.transpose(0, 2, 1, 3), FOX_HEAD_DIM ** -0.5, fcum)

        o = jnp.concatenate([o_a, o_b], axis=1).transpose(0, 2, 1, 3).reshape(b, s, MIX_WIDTH)
        x = x + gate1 * (o @ w_o)

        h2 = _modulate(x, norm2_g, shift2, scale2)
        x = x + gate2 * _peer(h2, w_pq, sub_keys, peer_u, peer_v)
    return x
```

```python
import functools
import math

import jax
import jax.numpy as jnp
from jax import lax
from jax.experimental import pallas as pl
from jax.experimental.pallas import tpu as pltpu

F32 = jnp.float32
BF16 = jnp.bfloat16

EPS = 1e-6
LOG2E = 1.4426950408889634
ROPE_THETA = 10000.0

LANES = 128
VMEM_LIMIT_BYTES = 56 * 1024 * 1024

MLA_HEADS = 8
MLA_Q_RANK = 384
MLA_KV_RANK = 256
MLA_NOPE = 64
MLA_ROPE = 32
MLA_QK = MLA_NOPE + MLA_ROPE
MLA_V = 64
FOX_HEADS = 8
FOX_DIM = 64
FOX_WIDTH = FOX_HEADS * FOX_DIM
HEADS = MLA_HEADS + FOX_HEADS
HEAD_PAD = 128
AUG_ROWS = 16
PEER_HEADS = 8
N_KEYS = 128
PEER_HALF = 128
PEER_TOPK = 16
OFF_CQ = 0
OFF_CKV = OFF_CQ + MLA_Q_RANK
OFF_KPE = OFF_CKV + MLA_KV_RANK
OFF_FQ = OFF_KPE + MLA_ROPE
OFF_FK = OFF_FQ + FOX_WIDTH
OFF_FV = OFF_FK + FOX_WIDTH
OFF_FL = OFF_FV + FOX_WIDTH

PREP_TOKENS = 512
ATTN_BLOCK = 512
PROJ_TOKENS = 512
PEER_TOKENS = 512
PEER_EXPERTS = 512

NT_DIMS = (((1,), (1,)), ((), ()))
TN_DIMS = (((0,), (0,)), ((), ()))


def _params(*semantics):
    return pltpu.CompilerParams(dimension_semantics=semantics, vmem_limit_bytes=VMEM_LIMIT_BYTES)


def _full(shape):
    n = len(shape)
    return pl.BlockSpec(shape, lambda *_: (0,) * n)


def _ada_kernel(c_ref, w_ref, b_ref, o_ref):
    c = c_ref[...]
    s = c / (1.0 + jnp.exp(-c))
    o_ref[...] = jnp.dot(s, w_ref[...], preferred_element_type=F32,
                         precision=lax.Precision.HIGHEST) + b_ref[...]


def _ada(c, w_ada, b_ada):
    b, d = c.shape
    n = w_ada.shape[1]
    rows = 8
    cp = jnp.zeros((rows, d), F32).at[:b].set(c.astype(F32))
    out = pl.pallas_call(
        _ada_kernel,
        grid=(n // d,),
        in_specs=[_full((rows, d)), pl.BlockSpec((d, d), lambda j: (0, j)), pl.BlockSpec((1, d), lambda j: (0, j))],
        out_specs=pl.BlockSpec((rows, d), lambda j: (0, j)),
        out_shape=jax.ShapeDtypeStruct((rows, n), F32),
        compiler_params=_params("arbitrary"),
        name="adaln",
    )(cp, w_ada.astype(F32), b_ada.astype(F32).reshape(1, n))
    return out[:b].reshape(b, n // d, d)


def _split3(f):
    hi = f.astype(BF16).astype(F32)
    r = f - hi
    mid = r.astype(BF16).astype(F32)
    return hi, mid, r - mid


def _aug_block(vals, ones_first, width):
    row = lax.broadcasted_iota(jnp.int32, (AUG_ROWS, width), 0)
    v0, v1, v2 = vals
    off = 3 if ones_first else 0
    blk = jnp.where(row == off, v0, jnp.where(row == off + 1, v1, jnp.where(row == off + 2, v2, 0.0)))
    one_lo = 0 if ones_first else 3
    return jnp.where((row >= one_lo) & (row < one_lo + 3), 1.0, blk)


def _prep_kernel(x_ref, pos_ref, mod_ref, g1_ref, winT_ref, gqa_ref, gkva_ref, wuqT_ref, wukvT_ref,
                 gq_ref, gk_ref, gfq_ref, gfk_ref, bf_ref, invf_ref,
                 qt_ref, k_ref, vt_ref, carry_ref):
    tm = x_ref.shape[1]

    @pl.when(pl.program_id(1) == 0)
    def _():
        carry_ref[...] = jnp.zeros_like(carry_ref)

    x = x_ref[0]
    y = x * lax.rsqrt(jnp.mean(x * x, axis=-1, keepdims=True) + EPS) * g1_ref[...]
    h = y * (1.0 + mod_ref[0, 1:2, :]) + mod_ref[0, 0:1, :]
    projT = lax.dot_general(winT_ref[...], h.astype(BF16), NT_DIMS, preferred_element_type=F32)

    def rms_rows(v, n):
        return lax.rsqrt(jnp.sum(v * v, axis=0, keepdims=True) * (1.0 / n) + EPS)

    cq = projT[OFF_CQ:OFF_CKV]
    cqn = (cq * rms_rows(cq, MLA_Q_RANK) * gqa_ref[...]).astype(BF16)
    qaT = jnp.dot(wuqT_ref[...], cqn, preferred_element_type=F32)
    ckv = projT[OFF_CKV:OFF_KPE]
    ckvn = (ckv * rms_rows(ckv, MLA_KV_RANK) * gkva_ref[...]).astype(BF16)
    kvT = jnp.dot(wukvT_ref[...], ckvn, preferred_element_type=F32)
    kpe = projT[OFF_KPE:OFF_FQ]

    ang = pos_ref[0].astype(F32) * invf_ref[...]
    cos, sin = jnp.cos(ang), jnp.sin(ang)
    half = MLA_ROPE // 2

    def rope(v):
        v1, v2 = v[:half], v[half:]
        return v1 * cos - v2 * sin, v2 * cos + v1 * sin

    gq, gk = gq_ref[...], gk_ref[...]
    kpe_ss = jnp.sum(kpe * kpe, axis=0, keepdims=True)
    kr1, kr2 = rope(kpe * gk[MLA_NOPE:])
    zpad_mla = jnp.zeros((HEAD_PAD - MLA_QK, tm), F32)
    q_scale = MLA_QK ** -0.5 * LOG2E
    for hd in range(MLA_HEADS):
        q = qaT[hd * MLA_QK:(hd + 1) * MLA_QK]
        qn = q * rms_rows(q, MLA_QK) * gq
        r1, r2 = rope(qn[MLA_NOPE:])
        qt_ref[0, hd] = (jnp.concatenate([qn[:MLA_NOPE], r1, r2, zpad_mla], axis=0) * q_scale).astype(BF16)
        base = hd * (MLA_NOPE + MLA_V)
        kn = kvT[base:base + MLA_NOPE]
        r = lax.rsqrt((jnp.sum(kn * kn, axis=0, keepdims=True) + kpe_ss) * (1.0 / MLA_QK) + EPS)
        kblk = jnp.concatenate([kn * r * gk[:MLA_NOPE], kr1 * r, kr2 * r, zpad_mla], axis=0)
        k_ref[0, hd] = kblk.T.astype(BF16)
        vt_ref[0, hd, 0] = kvT[base + MLA_NOPE:base + MLA_NOPE + MLA_V].astype(BF16)

    z = projT[OFF_FL:OFF_FL + FOX_HEADS] + bf_ref[...]
    logf = -(jnp.maximum(-z, 0.0) + jnp.log1p(jnp.exp(-jnp.abs(z))))
    lhi, lmid, llo = _split3(logf)
    tri = (lax.broadcasted_iota(jnp.int32, (tm, tm), 0) <= lax.broadcasted_iota(jnp.int32, (tm, tm), 1))
    tri = jnp.where(tri, 1.0, 0.0).astype(BF16)
    parts = jnp.dot(jnp.concatenate([lhi, lmid, llo], axis=0).astype(BF16), tri, preferred_element_type=F32)
    fcum = (parts[2 * FOX_HEADS:] + parts[FOX_HEADS:2 * FOX_HEADS]) + parts[:FOX_HEADS] + carry_ref[:, 0:1]
    carry_ref[...] = jnp.broadcast_to(fcum[:, tm - 1:tm], carry_ref.shape)
    fc = fcum * LOG2E

    gfq, gfk = gfq_ref[...], gfk_ref[...]
    zpad_fox = jnp.zeros((HEAD_PAD - FOX_DIM - AUG_ROWS, tm), F32)
    f_scale = FOX_DIM ** -0.5 * LOG2E
    for hd in range(FOX_HEADS):
        fq = projT[OFF_FQ + hd * FOX_DIM:OFF_FQ + (hd + 1) * FOX_DIM]
        fk = projT[OFF_FK + hd * FOX_DIM:OFF_FK + (hd + 1) * FOX_DIM]
        fv = projT[OFF_FV + hd * FOX_DIM:OFF_FV + (hd + 1) * FOX_DIM]
        f_row = fc[hd:hd + 1]
        qn = fq * rms_rows(fq, FOX_DIM) * gfq * f_scale
        qblk = jnp.concatenate([qn, _aug_block(_split3(f_row), False, tm), zpad_fox], axis=0)
        qt_ref[0, MLA_HEADS + hd] = qblk.astype(BF16)
        kn = fk * rms_rows(fk, FOX_DIM) * gfk
        kblk = jnp.concatenate([kn, _aug_block(_split3(-f_row), True, tm), zpad_fox], axis=0)
        k_ref[0, MLA_HEADS + hd] = kblk.T.astype(BF16)
        vt_ref[0, MLA_HEADS + hd, 0] = fv.astype(BF16)


def _prep(x, positions, mod, norm1_g, w_in, mla_qa_g, mla_kva_g, w_uq, w_ukv, mla_q_g, mla_k_g,
          fox_q_g, fox_k_g, b_f):
    b, s, d = x.shape
    tm = PREP_TOKENS
    nj = s // tm
    col = lambda v: v.astype(F32).reshape(-1, 1)
    inv_freq = ROPE_THETA ** (-jnp.arange(0, MLA_ROPE, 2, dtype=F32) / MLA_ROPE)
    in_width = w_in.shape[1]
    args = (x, positions.reshape(b, 1, s), mod, norm1_g.astype(F32).reshape(1, d), w_in.T.astype(BF16),
            col(mla_qa_g), col(mla_kva_g), w_uq.T.astype(BF16), w_ukv.T.astype(BF16),
            col(mla_q_g), col(mla_k_g), col(fox_q_g), col(fox_k_g), col(b_f), col(inv_freq))
    in_specs = [
        pl.BlockSpec((1, tm, d), lambda i, j: (i, j, 0)),
        pl.BlockSpec((1, 1, tm), lambda i, j: (i, 0, j)),
        pl.BlockSpec((1, 6, d), lambda i, j: (i, 0, 0)),
    ] + [_full(a.shape) for a in args[3:]]
    assert in_width == OFF_FL + FOX_HEADS
    return pl.pallas_call(
        _prep_kernel,
        grid=(b, nj),
        in_specs=in_specs,
        out_specs=[
            pl.BlockSpec((1, HEADS, HEAD_PAD, tm), lambda i, j: (i, 0, 0, j)),
            pl.BlockSpec((1, HEADS, tm, HEAD_PAD), lambda i, j: (i, 0, j, 0)),
            pl.BlockSpec((1, HEADS, 1, MLA_V, tm), lambda i, j: (i, 0, j, 0, 0)),
        ],
        out_shape=[
            jax.ShapeDtypeStruct((b, HEADS, HEAD_PAD, s), BF16),
            jax.ShapeDtypeStruct((b, HEADS, s, HEAD_PAD), BF16),
            jax.ShapeDtypeStruct((b, HEADS, nj, MLA_V, tm), BF16),
        ],
        scratch_shapes=[pltpu.VMEM((FOX_HEADS, LANES), F32)],
        compiler_params=_params("arbitrary", "arbitrary"),
        name="token_prep",
    )(*args)


def _attn_kernel(qt_ref, k_ref, vt_ref, o_ref):
    blk = qt_ref.shape[3]
    i = pl.program_id(2)
    qt = qt_ref[0, 0]

    def step(j, carry, diagonal):
        m, l, acc = carry
        kb = k_ref[0, 0, pl.ds(pl.multiple_of(j * blk, blk), blk), :]
        s = jnp.dot(kb, qt, preferred_element_type=F32)
        if diagonal:
            keep = lax.broadcasted_iota(jnp.int32, (blk, blk), 0) <= lax.broadcasted_iota(jnp.int32, (blk, blk), 1)
            s = jnp.where(keep, s, -jnp.inf)
        m_new = jnp.maximum(m, jnp.max(s, axis=0, keepdims=True))
        p = jnp.exp2(s - m_new)
        alpha = jnp.exp2(m - m_new)
        l = alpha * l + jnp.sum(p, axis=0, keepdims=True)
        acc = alpha * acc + jnp.dot(vt_ref[0, 0, j], p.astype(BF16), preferred_element_type=F32)
        return m_new, l, acc

    init = (jnp.full((1, blk), -jnp.inf, F32), jnp.zeros((1, blk), F32), jnp.zeros((MLA_V, blk), F32))
    carry = lax.fori_loop(0, i, lambda j, c: step(j, c, False), init)
    _, l, acc = step(i, carry, True)
    o_ref[0] = (acc / l).astype(o_ref.dtype)


def _attention(qt, k, vt):
    b, heads, _, s = qt.shape
    blk = ATTN_BLOCK
    assert vt.shape[4] == blk
    return pl.pallas_call(
        _attn_kernel,
        grid=(b, heads, s // blk),
        in_specs=[
            pl.BlockSpec((1, 1, HEAD_PAD, blk), lambda bi, h, i: (bi, h, 0, i)),
            pl.BlockSpec((1, 1, s, HEAD_PAD), lambda bi, h, i: (bi, h, 0, 0)),
            pl.BlockSpec((1, 1, s // blk, MLA_V, blk), lambda bi, h, i: (bi, h, 0, 0, 0)),
        ],
        out_specs=pl.BlockSpec((1, MLA_V, blk), lambda bi, h, i: (bi, h, i)),
        out_shape=jax.ShapeDtypeStruct((b, heads * MLA_V, s), BF16),
        compiler_params=_params("arbitrary", "arbitrary", "arbitrary"),
        name="flash_attention",
    )(qt, k, vt)


def _proj_kernel(ot_ref, x_ref, mod_ref, wo_ref, g2_ref, x1_ref, h2_ref):
    attn = lax.dot_general(ot_ref[0], wo_ref[...], TN_DIMS, preferred_element_type=F32)
    x1 = x_ref[0] + mod_ref[0, 2:3, :] * attn
    x1_ref[0] = x1
    y = x1 * lax.rsqrt(jnp.mean(x1 * x1, axis=-1, keepdims=True) + EPS) * g2_ref[...]
    h2_ref[0] = (y * (1.0 + mod_ref[0, 4:5, :]) + mod_ref[0, 3:4, :]).astype(h2_ref.dtype)


def _out_proj(ot, x, mod, w_o, norm2_g):
    b, s, d = x.shape
    tm = PROJ_TOKENS
    width = ot.shape[1]
    return pl.pallas_call(
        _proj_kernel,
        grid=(b, s // tm),
        in_specs=[
            pl.BlockSpec((1, width, tm), lambda i, j: (i, 0, j)),
            pl.BlockSpec((1, tm, d), lambda i, j: (i, j, 0)),
            pl.BlockSpec((1, 6, d), lambda i, j: (i, 0, 0)),
            _full((width, d)),
            _full((1, d)),
        ],
        out_specs=[pl.BlockSpec((1, tm, d), lambda i, j: (i, j, 0))] * 2,
        out_shape=[jax.ShapeDtypeStruct((b, s, d), F32), jax.ShapeDtypeStruct((b, s, d), BF16)],
        compiler_params=_params("arbitrary", "arbitrary"),
        name="out_proj",
    )(ot, x, mod, w_o.astype(BF16), norm2_g.astype(F32).reshape(1, d))


def _top_values(s, want_rank):
    work = s
    rank = jnp.full(s.shape, float(PEER_TOPK), F32) if want_rank else None
    vals = []
    for i in range(PEER_TOPK):
        m = jnp.max(work, axis=0, keepdims=True)
        hit = work == m
        if want_rank:
            rank = jnp.where(hit, float(i), rank)
        work = jnp.where(hit, -jnp.inf, work)
        vals.append(m)
    return jnp.concatenate(vals, axis=0), rank


def _route(s1, s2):
    v1, _ = _top_values(s1, False)
    v2, rank2 = _top_values(s2, True)
    cands = [v1[0:1] + v2]
    cands += [v1[i:i + 1] + v2[0:8] for i in range(1, 8)]
    cands += [v1[8:16] + v2[0:1]]
    cand = jnp.concatenate(cands, axis=0)
    top = v1[0:1] + v2[0:1]
    zsum = jnp.zeros_like(top)
    tau = top
    for _ in range(PEER_TOPK):
        tau = jnp.max(cand, axis=0, keepdims=True)
        zsum = zsum + jnp.exp(tau - top)
        cand = jnp.where(cand == tau, -jnp.inf, cand)
    count = jnp.zeros(s1.shape, F32)
    for j in range(PEER_TOPK):
        count = count + jnp.where(s1 + v2[j:j + 1] >= tau, 1.0, 0.0)
    pa = jnp.exp(s1 - v1[0:1]) / zsum
    pb = jnp.exp(s2 - v2[0:1])
    return rank2, count, pa, pb


def _gelu_tanh(a):
    return 0.5 * a * (1.0 + jnp.tanh(math.sqrt(2.0 / math.pi) * (a + 0.044715 * (a * a * a))))


def _peer_kernel(h2_ref, x1_ref, mod_ref, wpqT_ref, keys_ref, u_ref, vT_ref, o_ref,
                 rank_ref, cnt_ref, pa_ref, pb_ref, acc_ref):
    j = pl.program_id(1)
    eb = u_ref.shape[0]
    h2 = h2_ref[...]

    @pl.when(j == 0)
    def _():
        acc_ref[...] = jnp.zeros_like(acc_ref)
        qpT = lax.dot_general(wpqT_ref[...], h2, NT_DIMS, preferred_element_type=F32)
        for hd in range(PEER_HEADS):
            sc = []
            for half in range(2):
                lo = (hd * 2 + half) * PEER_HALF
                sc.append(jnp.dot(keys_ref[hd, half], qpT[lo:lo + PEER_HALF], preferred_element_type=F32,
                                  precision=lax.Precision.HIGHEST))
            rank2, count, pa, pb = _route(sc[0], sc[1])
            rank_ref[hd], cnt_ref[hd], pa_ref[hd], pb_ref[hd] = rank2, count, pa, pb

    aT = lax.dot_general(u_ref[...], h2, NT_DIMS, preferred_element_type=F32)
    rows = eb // N_KEYS
    zs = []
    for r in range(rows):
        e1 = j * rows + r
        w = jnp.zeros((N_KEYS, h2.shape[0]), F32)
        for hd in range(PEER_HEADS):
            cnt = cnt_ref[hd, pl.ds(e1, 1), :]
            pa = pa_ref[hd, pl.ds(e1, 1), :]
            w = w + jnp.where(rank_ref[hd] < cnt, pb_ref[hd], 0.0) * pa
        zs.append((_gelu_tanh(aT[r * N_KEYS:(r + 1) * N_KEYS]) * w).astype(BF16))
    zT = jnp.concatenate(zs, axis=0)
    acc_ref[...] += jnp.dot(vT_ref[...], zT, preferred_element_type=F32)

    @pl.when(j == pl.num_programs(1) - 1)
    def _():
        o_ref[...] = x1_ref[...] + mod_ref[0, 5:6, :] * acc_ref[...].T


def _peer(h2, x1, mod, w_pq, sub_keys, peer_u, peer_v, seq):
    t, d = h2.shape
    tb, eb = PEER_TOKENS, PEER_EXPERTS
    n_exp = peer_u.shape[0]
    table = pltpu.VMEM((PEER_HEADS, N_KEYS, tb), F32)
    return pl.pallas_call(
        _peer_kernel,
        grid=(t // tb, n_exp // eb),
        in_specs=[
            pl.BlockSpec((tb, d), lambda i, j: (i, 0)),
            pl.BlockSpec((tb, d), lambda i, j: (i, 0)),
            pl.BlockSpec((1, 6, d), lambda i, j: (i * tb // seq, 0, 0)),
            _full((w_pq.shape[1], d)),
            _full(sub_keys.shape),
            pl.BlockSpec((eb, d), lambda i, j: (j, 0)),
            pl.BlockSpec((d, eb), lambda i, j: (0, j)),
        ],
        out_specs=pl.BlockSpec((tb, d), lambda i, j: (i, 0)),
        out_shape=jax.ShapeDtypeStruct((t, d), F32),
        scratch_shapes=[table, table, table, table, pltpu.VMEM((d, tb), F32)],
        compiler_params=_params("arbitrary", "arbitrary"),
        name="peer",
    )(h2, x1, mod, w_pq.T.astype(BF16), sub_keys.astype(F32), peer_u.astype(BF16), peer_v.T.astype(BF16))


def kernel(x, c, positions, w_ada, b_ada, norm1_g, w_in, mla_qa_g, mla_kva_g, w_uq, w_ukv, mla_q_g, mla_k_g,
           fox_q_g, fox_k_g, b_f, w_o, norm2_g, w_pq, sub_keys, peer_u, peer_v):
    b, s, d = x.shape
    assert s % PREP_TOKENS == 0 and s % ATTN_BLOCK == 0 and s % PROJ_TOKENS == 0 and s % PEER_TOKENS == 0
    assert PREP_TOKENS == ATTN_BLOCK and peer_u.shape[0] == N_KEYS * N_KEYS
    mod = _ada(c, w_ada, b_ada)
    qt, k, vt = _prep(x, positions, mod, norm1_g, w_in, mla_qa_g, mla_kva_g, w_uq, w_ukv, mla_q_g, mla_k_g,
                      fox_q_g, fox_k_g, b_f)
    ot = _attention(qt, k, vt)
    x1, h2 = _out_proj(ot, x, mod, w_o, norm2_g)
    out = _peer(h2.reshape(b * s, d), x1.reshape(b * s, d), mod, w_pq, sub_keys, peer_u, peer_v, s)
    return out.reshape(b, s, d).astype(x.dtype)
```

```python
import functools
import math

import jax
import jax.numpy as jnp
from jax import lax
from jax.experimental import pallas as pl
from jax.experimental.pallas import tpu as pltpu

F32 = jnp.float32
BF16 = jnp.bfloat16

EPS = 1e-6
LOG2E = 1.4426950408889634
ROPE_THETA = 10000.0

LANES = 128
VMEM_LIMIT_BYTES = 56 * 1024 * 1024

MLA_HEADS = 8
MLA_Q_RANK = 384
MLA_KV_RANK = 256
MLA_NOPE = 64
MLA_ROPE = 32
MLA_QK = MLA_NOPE + MLA_ROPE
MLA_V = 64
FOX_HEADS = 8
FOX_DIM = 64
FOX_WIDTH = FOX_HEADS * FOX_DIM
HEADS = MLA_HEADS + FOX_HEADS
HEAD_PAD = 128
AUG_ROWS = 16
PEER_HEADS = 8
N_KEYS = 128
PEER_HALF = 128
PEER_TOPK = 16
OFF_CQ = 0
OFF_CKV = OFF_CQ + MLA_Q_RANK
OFF_KPE = OFF_CKV + MLA_KV_RANK
OFF_FQ = OFF_KPE + MLA_ROPE
OFF_FK = OFF_FQ + FOX_WIDTH
OFF_FV = OFF_FK + FOX_WIDTH
OFF_FL = OFF_FV + FOX_WIDTH

PREP_TOKENS = 512
ATTN_BLOCK = 512
PROJ_TOKENS = 512
PEER_TOKENS = 512
PEER_EXPERTS = 2048
PEER_SUB = 512
BF16_ROWS = 16

NT_DIMS = (((1,), (1,)), ((), ()))
TN_DIMS = (((0,), (0,)), ((), ()))


def _params(*semantics):
    return pltpu.CompilerParams(dimension_semantics=semantics, vmem_limit_bytes=VMEM_LIMIT_BYTES)


def _full(shape):
    n = len(shape)
    return pl.BlockSpec(shape, lambda *_: (0,) * n)


def _ada_kernel(c_ref, w_ref, b_ref, o_ref):
    c = c_ref[...]
    s = c / (1.0 + jnp.exp(-c))
    o_ref[...] = jnp.dot(s, w_ref[...], preferred_element_type=F32,
                         precision=lax.Precision.HIGHEST) + b_ref[...]


def _ada(c, w_ada, b_ada):
    b, d = c.shape
    n = w_ada.shape[1]
    rows = 8
    cp = jnp.zeros((rows, d), F32).at[:b].set(c.astype(F32))
    out = pl.pallas_call(
        _ada_kernel,
        grid=(n // d,),
        in_specs=[_full((rows, d)), pl.BlockSpec((d, d), lambda j: (0, j)), pl.BlockSpec((1, d), lambda j: (0, j))],
        out_specs=pl.BlockSpec((rows, d), lambda j: (0, j)),
        out_shape=jax.ShapeDtypeStruct((rows, n), F32),
        compiler_params=_params("arbitrary"),
        name="adaln",
    )(cp, w_ada.astype(F32), b_ada.astype(F32).reshape(1, n))
    return out[:b].reshape(b, n // d, d)


def _split3(f):
    hi = f.astype(BF16).astype(F32)
    r = f - hi
    mid = r.astype(BF16).astype(F32)
    return hi, mid, r - mid


def _aug_block(vals, ones_first, width):
    row = lax.broadcasted_iota(jnp.int32, (AUG_ROWS, width), 0)
    v0, v1, v2 = vals
    off = 3 if ones_first else 0
    blk = jnp.where(row == off, v0, jnp.where(row == off + 1, v1, jnp.where(row == off + 2, v2, 0.0)))
    one_lo = 0 if ones_first else 3
    return jnp.where((row >= one_lo) & (row < one_lo + 3), 1.0, blk)


def _prep_kernel(x_ref, pos_ref, mod_ref, g1_ref, winT_ref, gqa_ref, gkva_ref, wuqT_ref, wukvT_ref,
                 gq_ref, gk_ref, gfq_ref, gfk_ref, bf_ref, invf_ref,
                 qt_ref, k_ref, vt_ref, carry_ref):
    tm = x_ref.shape[1]

    @pl.when(pl.program_id(1) == 0)
    def _():
        carry_ref[...] = jnp.zeros_like(carry_ref)

    x = x_ref[0]
    y = x * lax.rsqrt(jnp.mean(x * x, axis=-1, keepdims=True) + EPS) * g1_ref[...]
    h = y * (1.0 + mod_ref[0, 1:2, :]) + mod_ref[0, 0:1, :]
    projT = lax.dot_general(winT_ref[...], h.astype(BF16), NT_DIMS, preferred_element_type=F32)

    def rms_rows(v, n):
        return lax.rsqrt(jnp.sum(v * v, axis=0, keepdims=True) * (1.0 / n) + EPS)

    cq = projT[OFF_CQ:OFF_CKV]
    cqn = (cq * rms_rows(cq, MLA_Q_RANK) * gqa_ref[...]).astype(BF16)
    qaT = jnp.dot(wuqT_ref[...], cqn, preferred_element_type=F32)
    ckv = projT[OFF_CKV:OFF_KPE]
    ckvn = (ckv * rms_rows(ckv, MLA_KV_RANK) * gkva_ref[...]).astype(BF16)
    kvT = jnp.dot(wukvT_ref[...], ckvn, preferred_element_type=F32)
    kpe = projT[OFF_KPE:OFF_FQ]

    ang = pos_ref[0].astype(F32) * invf_ref[...]
    cos, sin = jnp.cos(ang), jnp.sin(ang)
    half = MLA_ROPE // 2

    def rope(v):
        v1, v2 = v[:half], v[half:]
        return v1 * cos - v2 * sin, v2 * cos + v1 * sin

    gq, gk = gq_ref[...], gk_ref[...]
    kpe_ss = jnp.sum(kpe * kpe, axis=0, keepdims=True)
    kr1, kr2 = rope(kpe * gk[MLA_NOPE:])
    zpad_mla = jnp.zeros((HEAD_PAD - MLA_QK, tm), F32)
    q_scale = MLA_QK ** -0.5 * LOG2E
    for hd in range(MLA_HEADS):
        q = qaT[hd * MLA_QK:(hd + 1) * MLA_QK]
        qn = q * rms_rows(q, MLA_QK) * gq
        r1, r2 = rope(qn[MLA_NOPE:])
        qt_ref[0, hd] = (jnp.concatenate([qn[:MLA_NOPE], r1, r2, zpad_mla], axis=0) * q_scale).astype(BF16)
        base = hd * (MLA_NOPE + MLA_V)
        kn = kvT[base:base + MLA_NOPE]
        r = lax.rsqrt((jnp.sum(kn * kn, axis=0, keepdims=True) + kpe_ss) * (1.0 / MLA_QK) + EPS)
        kblk = jnp.concatenate([kn * r * gk[:MLA_NOPE], kr1 * r, kr2 * r, zpad_mla], axis=0)
        k_ref[0, hd] = kblk.T.astype(BF16)
        vt_ref[0, hd, 0] = kvT[base + MLA_NOPE:base + MLA_NOPE + MLA_V].astype(BF16)

    z = projT[OFF_FL:OFF_FL + FOX_HEADS] + bf_ref[...]
    logf = -(jnp.maximum(-z, 0.0) + jnp.log1p(jnp.exp(-jnp.abs(z))))
    lhi, lmid, llo = _split3(logf)
    tri = (lax.broadcasted_iota(jnp.int32, (tm, tm), 0) <= lax.broadcasted_iota(jnp.int32, (tm, tm), 1))
    tri = jnp.where(tri, 1.0, 0.0).astype(BF16)
    parts = jnp.dot(jnp.concatenate([lhi, lmid, llo], axis=0).astype(BF16), tri, preferred_element_type=F32)
    fcum = (parts[2 * FOX_HEADS:] + parts[FOX_HEADS:2 * FOX_HEADS]) + parts[:FOX_HEADS] + carry_ref[:, 0:1]
    carry_ref[...] = jnp.broadcast_to(fcum[:, tm - 1:tm], carry_ref.shape)
    fc = fcum * LOG2E

    gfq, gfk = gfq_ref[...], gfk_ref[...]
    zpad_fox = jnp.zeros((HEAD_PAD - FOX_DIM - AUG_ROWS, tm), F32)
    f_scale = FOX_DIM ** -0.5 * LOG2E
    for hd in range(FOX_HEADS):
        fq = projT[OFF_FQ + hd * FOX_DIM:OFF_FQ + (hd + 1) * FOX_DIM]
        fk = projT[OFF_FK + hd * FOX_DIM:OFF_FK + (hd + 1) * FOX_DIM]
        fv = projT[OFF_FV + hd * FOX_DIM:OFF_FV + (hd + 1) * FOX_DIM]
        f_row = fc[hd:hd + 1]
        qn = fq * rms_rows(fq, FOX_DIM) * gfq * f_scale
        qblk = jnp.concatenate([qn, _aug_block(_split3(f_row), False, tm), zpad_fox], axis=0)
        qt_ref[0, MLA_HEADS + hd] = qblk.astype(BF16)
        kn = fk * rms_rows(fk, FOX_DIM) * gfk
        kblk = jnp.concatenate([kn, _aug_block(_split3(-f_row), True, tm), zpad_fox], axis=0)
        k_ref[0, MLA_HEADS + hd] = kblk.T.astype(BF16)
        vt_ref[0, MLA_HEADS + hd, 0] = fv.astype(BF16)


def _prep(x, positions, mod, norm1_g, w_in, mla_qa_g, mla_kva_g, w_uq, w_ukv, mla_q_g, mla_k_g,
          fox_q_g, fox_k_g, b_f):
    b, s, d = x.shape
    tm = PREP_TOKENS
    nj = s // tm
    col = lambda v: v.astype(F32).reshape(-1, 1)
    inv_freq = ROPE_THETA ** (-jnp.arange(0, MLA_ROPE, 2, dtype=F32) / MLA_ROPE)
    in_width = w_in.shape[1]
    args = (x, positions.reshape(b, 1, s), mod, norm1_g.astype(F32).reshape(1, d), w_in.T.astype(BF16),
            col(mla_qa_g), col(mla_kva_g), w_uq.T.astype(BF16), w_ukv.T.astype(BF16),
            col(mla_q_g), col(mla_k_g), col(fox_q_g), col(fox_k_g), col(b_f), col(inv_freq))
    in_specs = [
        pl.BlockSpec((1, tm, d), lambda i, j: (i, j, 0)),
        pl.BlockSpec((1, 1, tm), lambda i, j: (i, 0, j)),
        pl.BlockSpec((1, 6, d), lambda i, j: (i, 0, 0)),
    ] + [_full(a.shape) for a in args[3:]]
    assert in_width == OFF_FL + FOX_HEADS
    return pl.pallas_call(
        _prep_kernel,
        grid=(b, nj),
        in_specs=in_specs,
        out_specs=[
            pl.BlockSpec((1, HEADS, HEAD_PAD, tm), lambda i, j: (i, 0, 0, j)),
            pl.BlockSpec((1, HEADS, tm, HEAD_PAD), lambda i, j: (i, 0, j, 0)),
            pl.BlockSpec((1, HEADS, 1, MLA_V, tm), lambda i, j: (i, 0, j, 0, 0)),
        ],
        out_shape=[
            jax.ShapeDtypeStruct((b, HEADS, HEAD_PAD, s), BF16),
            jax.ShapeDtypeStruct((b, HEADS, s, HEAD_PAD), BF16),
            jax.ShapeDtypeStruct((b, HEADS, nj, MLA_V, tm), BF16),
        ],
        scratch_shapes=[pltpu.VMEM((FOX_HEADS, LANES), F32)],
        compiler_params=_params("arbitrary", "arbitrary"),
        name="token_prep",
    )(*args)


def _attn_kernel(qt_ref, k_ref, vt_ref, o_ref):
    blk = qt_ref.shape[3]
    i = pl.program_id(2)
    qt = qt_ref[0, 0]

    def step(j, carry, diagonal):
        m, l, acc = carry
        kb = k_ref[0, 0, pl.ds(pl.multiple_of(j * blk, blk), blk), :]
        s = jnp.dot(kb, qt, preferred_element_type=F32)
        if diagonal:
            keep = lax.broadcasted_iota(jnp.int32, (blk, blk), 0) <= lax.broadcasted_iota(jnp.int32, (blk, blk), 1)
            s = jnp.where(keep, s, -jnp.inf)
        m_new = jnp.maximum(m, jnp.max(s, axis=0, keepdims=True))
        p = jnp.exp2(s - m_new)
        alpha = jnp.exp2(m - m_new)
        l = alpha * l + jnp.sum(p, axis=0, keepdims=True)
        acc = alpha * acc + jnp.dot(vt_ref[0, 0, j], p.astype(BF16), preferred_element_type=F32)
        return m_new, l, acc

    init = (jnp.full((1, blk), -jnp.inf, F32), jnp.zeros((1, blk), F32), jnp.zeros((MLA_V, blk), F32))
    carry = lax.fori_loop(0, i, lambda j, c: step(j, c, False), init)
    _, l, acc = step(i, carry, True)
    o_ref[0] = (acc / l).astype(o_ref.dtype)


def _attention(qt, k, vt):
    b, heads, _, s = qt.shape
    blk = ATTN_BLOCK
    assert vt.shape[4] == blk
    return pl.pallas_call(
        _attn_kernel,
        grid=(b, heads, s // blk),
        in_specs=[
            pl.BlockSpec((1, 1, HEAD_PAD, blk), lambda bi, h, i: (bi, h, 0, i)),
            pl.BlockSpec((1, 1, s, HEAD_PAD), lambda bi, h, i: (bi, h, 0, 0)),
            pl.BlockSpec((1, 1, s // blk, MLA_V, blk), lambda bi, h, i: (bi, h, 0, 0, 0)),
        ],
        out_specs=pl.BlockSpec((1, MLA_V, blk), lambda bi, h, i: (bi, h, i)),
        out_shape=jax.ShapeDtypeStruct((b, heads * MLA_V, s), BF16),
        compiler_params=_params("arbitrary", "arbitrary", "arbitrary"),
        name="flash_attention",
    )(qt, k, vt)


def _proj_kernel(ot_ref, x_ref, mod_ref, wo_ref, g2_ref, x1_ref, h2_ref):
    attn = lax.dot_general(ot_ref[0], wo_ref[...], TN_DIMS, preferred_element_type=F32)
    x1 = x_ref[0] + mod_ref[0, 2:3, :] * attn
    x1_ref[0] = x1
    y = x1 * lax.rsqrt(jnp.mean(x1 * x1, axis=-1, keepdims=True) + EPS) * g2_ref[...]
    h2_ref[0] = (y * (1.0 + mod_ref[0, 4:5, :]) + mod_ref[0, 3:4, :]).astype(h2_ref.dtype)


def _out_proj(ot, x, mod, w_o, norm2_g):
    b, s, d = x.shape
    tm = PROJ_TOKENS
    width = ot.shape[1]
    return pl.pallas_call(
        _proj_kernel,
        grid=(b, s // tm),
        in_specs=[
            pl.BlockSpec((1, width, tm), lambda i, j: (i, 0, j)),
            pl.BlockSpec((1, tm, d), lambda i, j: (i, j, 0)),
            pl.BlockSpec((1, 6, d), lambda i, j: (i, 0, 0)),
            _full((width, d)),
            _full((1, d)),
        ],
        out_specs=[pl.BlockSpec((1, tm, d), lambda i, j: (i, j, 0))] * 2,
        out_shape=[jax.ShapeDtypeStruct((b, s, d), F32), jax.ShapeDtypeStruct((b, s, d), BF16)],
        compiler_params=_params("arbitrary", "arbitrary"),
        name="out_proj",
    )(ot, x, mod, w_o.astype(BF16), norm2_g.astype(F32).reshape(1, d))


def _top_values(s, want_rank):
    work = s
    rank = jnp.full(s.shape, float(PEER_TOPK), F32) if want_rank else None
    vals = []
    for i in range(PEER_TOPK):
        m = jnp.max(work, axis=0, keepdims=True)
        hit = work == m
        if want_rank:
            rank = jnp.where(hit, float(i), rank)
        work = jnp.where(hit, -jnp.inf, work)
        vals.append(m)
    return jnp.concatenate(vals, axis=0), rank


def _route(s1, s2):
    v1, _ = _top_values(s1, False)
    v2, rank2 = _top_values(s2, True)
    cands = [v1[0:1] + v2]
    cands += [v1[i:i + 1] + v2[0:8] for i in range(1, 8)]
    cands += [v1[8:16] + v2[0:1]]
    cand = jnp.concatenate(cands, axis=0)
    top = v1[0:1] + v2[0:1]
    zsum = jnp.zeros_like(top)
    tau = top
    for _ in range(PEER_TOPK):
        tau = jnp.max(cand, axis=0, keepdims=True)
        zsum = zsum + jnp.exp(tau - top)
        cand = jnp.where(cand == tau, -jnp.inf, cand)
    count = jnp.zeros(s1.shape, F32)
    for j in range(PEER_TOPK):
        count = count + jnp.where(s1 + v2[j:j + 1] >= tau, 1.0, 0.0)
    pa_half = jnp.exp(s1 - v1[0:1]) * (0.5 / zsum)
    pb = jnp.exp(s2 - v2[0:1])
    return rank2, count, pa_half, pb


GELU_C1 = math.sqrt(2.0 / math.pi)
GELU_C2 = GELU_C1 * 0.044715


def _gelu_times_two(a):
    t = jnp.tanh(a * (GELU_C1 + GELU_C2 * (a * a)))
    return a + a * t


def _peer_kernel(h2_ref, x1_ref, mod_ref, wpqT_ref, keys_ref, u_ref, vT_ref, o_ref,
                 rank_ref, pb_ref, cnt_ref, pa_ref, sc_ref, acc_ref):
    j = pl.program_id(1)
    tb = h2_ref.shape[0]
    subs = u_ref.shape[0] // PEER_SUB
    chunks = tb // LANES
    tiles = N_KEYS // BF16_ROWS
    h2 = h2_ref[...]

    @pl.when(j == 0)
    def _():
        acc_ref[...] = jnp.zeros_like(acc_ref)

        def head_body(hd, carry):
            w_rows = wpqT_ref[pl.ds(pl.multiple_of(hd * 2 * PEER_HALF, 2 * PEER_HALF), 2 * PEER_HALF), :]
            qpT = lax.dot_general(w_rows, h2, NT_DIMS, preferred_element_type=F32)
            for half in range(2):
                sc = jnp.dot(keys_ref[hd, half], qpT[half * PEER_HALF:(half + 1) * PEER_HALF],
                             preferred_element_type=F32, precision=lax.Precision.HIGHEST)
                for c in range(chunks):
                    sc_ref[half, c] = sc[:, c * LANES:(c + 1) * LANES]

            def chunk_body(c, carry2):
                rank2, count, pa_half, pb = _route(sc_ref[0, c], sc_ref[1, c])
                rank_ref[hd, c] = rank2.astype(BF16).reshape(tiles, BF16_ROWS, LANES)
                pb_ref[hd, c] = pb.astype(BF16).reshape(tiles, BF16_ROWS, LANES)
                cnt_ref[hd, c] = count
                pa_ref[hd, c] = pa_half
                return carry2

            return lax.fori_loop(0, chunks, chunk_body, carry)

        lax.fori_loop(0, PEER_HEADS, head_body, 0)

    rows = PEER_SUB // N_KEYS
    zero = jnp.zeros((), BF16)

    def first_matmul(sb):
        return lax.dot_general(u_ref[sb * PEER_SUB:(sb + 1) * PEER_SUB, :], h2, NT_DIMS,
                               preferred_element_type=F32)

    a_next = first_matmul(0)
    for sb in range(subs):
        aT = a_next
        if sb + 1 < subs:
            a_next = first_matmul(sb + 1)
        e1 = [(j * subs + sb) * rows + r for r in range(rows)]
        w_cols = [[] for _ in range(rows)]
        for c in range(chunks):
            w = [jnp.zeros((tiles, BF16_ROWS, LANES), BF16) for _ in range(rows)]
            for hd in range(PEER_HEADS):
                rk, pb = rank_ref[hd, c], pb_ref[hd, c]
                for r in range(rows):
                    cnt = jnp.broadcast_to(cnt_ref[hd, c, pl.ds(e1[r], 1), :], (BF16_ROWS, LANES)).astype(BF16)
                    pa = jnp.broadcast_to(pa_ref[hd, c, pl.ds(e1[r], 1), :], (BF16_ROWS, LANES)).astype(BF16)
                    w[r] = w[r] + jnp.where(rk < cnt[None], pb, zero) * pa[None]
            for r in range(rows):
                w_cols[r].append(w[r].reshape(N_KEYS, LANES))
        zs = []
        for r in range(rows):
            g = _gelu_times_two(aT[r * N_KEYS:(r + 1) * N_KEYS]).astype(BF16)
            zs.append(g * jnp.concatenate(w_cols[r], axis=1))
        zT = jnp.concatenate(zs, axis=0)
        acc_ref[...] += jnp.dot(vT_ref[:, sb * PEER_SUB:(sb + 1) * PEER_SUB], zT,
                                preferred_element_type=F32)

    @pl.when(j == pl.num_programs(1) - 1)
    def _():
        o_ref[...] = x1_ref[...] + mod_ref[0, 5:6, :] * acc_ref[...].T


def _peer(h2, x1, mod, w_pq, sub_keys, peer_u, peer_v, seq):
    t, d = h2.shape
    tb, eb = PEER_TOKENS, PEER_EXPERTS
    n_exp = peer_u.shape[0]
    assert eb % PEER_SUB == 0 and PEER_SUB % N_KEYS == 0
    chunks = tb // LANES
    packed = pltpu.VMEM((PEER_HEADS, chunks, N_KEYS // BF16_ROWS, BF16_ROWS, LANES), BF16)
    rowtab = pltpu.VMEM((PEER_HEADS, chunks, N_KEYS, LANES), F32)
    scores = pltpu.VMEM((2, chunks, N_KEYS, LANES), F32)
    return pl.pallas_call(
        _peer_kernel,
        grid=(t // tb, n_exp // eb),
        in_specs=[
            pl.BlockSpec((tb, d), lambda i, j: (i, 0)),
            pl.BlockSpec((tb, d), lambda i, j: (i, 0)),
            pl.BlockSpec((1, 6, d), lambda i, j: (i * tb // seq, 0, 0)),
            _full((w_pq.shape[1], d)),
            _full(sub_keys.shape),
            pl.BlockSpec((eb, d), lambda i, j: (j, 0)),
            pl.BlockSpec((d, eb), lambda i, j: (0, j)),
        ],
        out_specs=pl.BlockSpec((tb, d), lambda i, j: (i, 0)),
        out_shape=jax.ShapeDtypeStruct((t, d), F32),
        scratch_shapes=[packed, packed, rowtab, rowtab, scores, pltpu.VMEM((d, tb), F32)],
        compiler_params=_params("arbitrary", "arbitrary"),
        name="peer",
    )(h2, x1, mod, w_pq.T.astype(BF16), sub_keys.astype(F32), peer_u.astype(BF16), peer_v.T.astype(BF16))


def kernel(x, c, positions, w_ada, b_ada, norm1_g, w_in, mla_qa_g, mla_kva_g, w_uq, w_ukv, mla_q_g, mla_k_g,
           fox_q_g, fox_k_g, b_f, w_o, norm2_g, w_pq, sub_keys, peer_u, peer_v):
    b, s, d = x.shape
    assert s % PREP_TOKENS == 0 and s % ATTN_BLOCK == 0 and s % PROJ_TOKENS == 0 and s % PEER_TOKENS == 0
    assert PREP_TOKENS == ATTN_BLOCK and peer_u.shape[0] == N_KEYS * N_KEYS
    mod = _ada(c, w_ada, b_ada)
    qt, k, vt = _prep(x, positions, mod, norm1_g, w_in, mla_qa_g, mla_kva_g, w_uq, w_ukv, mla_q_g, mla_k_g,
                      fox_q_g, fox_k_g, b_f)
    ot = _attention(qt, k, vt)
    x1, h2 = _out_proj(ot, x, mod, w_o, norm2_g)
    out = _peer(h2.reshape(b * s, d), x1.reshape(b * s, d), mod, w_pq, sub_keys, peer_u, peer_v, s)
    return out.reshape(b, s, d).astype(x.dtype)
```

```python
import functools
import math

import jax
import jax.numpy as jnp
from jax import lax
from jax.experimental import pallas as pl
from jax.experimental.pallas import tpu as pltpu

F32 = jnp.float32
BF16 = jnp.bfloat16

EPS = 1e-6
LOG2E = 1.4426950408889634
ROPE_THETA = 10000.0

LANES = 128
VMEM_LIMIT_BYTES = 56 * 1024 * 1024

MLA_HEADS = 8
MLA_Q_RANK = 384
MLA_KV_RANK = 256
MLA_NOPE = 64
MLA_ROPE = 32
MLA_QK = MLA_NOPE + MLA_ROPE
MLA_V = 64
FOX_HEADS = 8
FOX_DIM = 64
FOX_WIDTH = FOX_HEADS * FOX_DIM
HEADS = MLA_HEADS + FOX_HEADS
HEAD_PAD = 128
AUG_ROWS = 16
V_ROWS = 80
PEER_HEADS = 8
N_KEYS = 128
PEER_HALF = 128
PEER_TOPK = 16
OFF_CQ = 0
OFF_CKV = OFF_CQ + MLA_Q_RANK
OFF_KPE = OFF_CKV + MLA_KV_RANK
OFF_FQ = OFF_KPE + MLA_ROPE
OFF_FK = OFF_FQ + FOX_WIDTH
OFF_FV = OFF_FK + FOX_WIDTH
OFF_FL = OFF_FV + FOX_WIDTH

PREP_TOKENS = 512
ATTN_BLOCK = 512
PROJ_TOKENS = 512
PEER_TOKENS = 512
PEER_EXPERTS = 2048
PEER_SUB = 512
BF16_ROWS = 16

NT_DIMS = (((1,), (1,)), ((), ()))
TN_DIMS = (((0,), (0,)), ((), ()))


def _params(*semantics):
    return pltpu.CompilerParams(dimension_semantics=semantics, vmem_limit_bytes=VMEM_LIMIT_BYTES)


def _full(shape):
    n = len(shape)
    return pl.BlockSpec(shape, lambda *_: (0,) * n)


def _ada_kernel(c_ref, w_ref, b_ref, o_ref):
    c = c_ref[...]
    s = c / (1.0 + jnp.exp(-c))
    o_ref[...] = jnp.dot(s, w_ref[...], preferred_element_type=F32,
                         precision=lax.Precision.HIGHEST) + b_ref[...]


def _ada(c, w_ada, b_ada):
    b, d = c.shape
    n = w_ada.shape[1]
    rows = 8
    cp = jnp.zeros((rows, d), F32).at[:b].set(c.astype(F32))
    out = pl.pallas_call(
        _ada_kernel,
        grid=(n // d,),
        in_specs=[_full((rows, d)), pl.BlockSpec((d, d), lambda j: (0, j)), pl.BlockSpec((1, d), lambda j: (0, j))],
        out_specs=pl.BlockSpec((rows, d), lambda j: (0, j)),
        out_shape=jax.ShapeDtypeStruct((rows, n), F32),
        compiler_params=_params("arbitrary"),
        name="adaln",
    )(cp, w_ada.astype(F32), b_ada.astype(F32).reshape(1, n))
    return out[:b].reshape(b, n // d, d)


def _split3(f):
    hi = f.astype(BF16).astype(F32)
    r = f - hi
    mid = r.astype(BF16).astype(F32)
    return hi, mid, r - mid


def _aug_block(vals, ones_first, width):
    row = lax.broadcasted_iota(jnp.int32, (AUG_ROWS, width), 0)
    v0, v1, v2 = vals
    off = 3 if ones_first else 0
    blk = jnp.where(row == off, v0, jnp.where(row == off + 1, v1, jnp.where(row == off + 2, v2, 0.0)))
    one_lo = 0 if ones_first else 3
    return jnp.where((row >= one_lo) & (row < one_lo + 3), 1.0, blk)


def _prep_kernel(x_ref, pos_ref, mod_ref, g1_ref, winT_ref, gqa_ref, gkva_ref, wuqT_ref, wukvT_ref,
                 gq_ref, gk_ref, gfq_ref, gfk_ref, bf_ref, invf_ref,
                 qt_ref, k_ref, vt_ref, carry_ref):
    tm = x_ref.shape[1]

    @pl.when(pl.program_id(1) == 0)
    def _():
        carry_ref[...] = jnp.zeros_like(carry_ref)

    x = x_ref[0]
    y = x * lax.rsqrt(jnp.mean(x * x, axis=-1, keepdims=True) + EPS) * g1_ref[...]
    h = y * (1.0 + mod_ref[0, 1:2, :]) + mod_ref[0, 0:1, :]
    projT = lax.dot_general(winT_ref[...], h.astype(BF16), NT_DIMS, preferred_element_type=F32)

    def rms_rows(v, n):
        return lax.rsqrt(jnp.sum(v * v, axis=0, keepdims=True) * (1.0 / n) + EPS)

    cq = projT[OFF_CQ:OFF_CKV]
    cqn = (cq * rms_rows(cq, MLA_Q_RANK) * gqa_ref[...]).astype(BF16)
    qaT = jnp.dot(wuqT_ref[...], cqn, preferred_element_type=F32)
    ckv = projT[OFF_CKV:OFF_KPE]
    ckvn = (ckv * rms_rows(ckv, MLA_KV_RANK) * gkva_ref[...]).astype(BF16)
    kvT = jnp.dot(wukvT_ref[...], ckvn, preferred_element_type=F32)
    kpe = projT[OFF_KPE:OFF_FQ]

    ang = pos_ref[0].astype(F32) * invf_ref[...]
    cos, sin = jnp.cos(ang), jnp.sin(ang)
    half = MLA_ROPE // 2

    def rope(v):
        v1, v2 = v[:half], v[half:]
        return v1 * cos - v2 * sin, v2 * cos + v1 * sin

    gq, gk = gq_ref[...], gk_ref[...]
    kpe_ss = jnp.sum(kpe * kpe, axis=0, keepdims=True)
    kr1, kr2 = rope(kpe * gk[MLA_NOPE:])
    zpad_mla = jnp.zeros((HEAD_PAD - MLA_QK, tm), F32)
    ones_rows = jnp.ones((V_ROWS - MLA_V, tm), BF16)
    q_scale = MLA_QK ** -0.5 * LOG2E
    for hd in range(MLA_HEADS):
        q = qaT[hd * MLA_QK:(hd + 1) * MLA_QK]
        qn = q * rms_rows(q, MLA_QK) * gq
        r1, r2 = rope(qn[MLA_NOPE:])
        qt_ref[0, hd] = (jnp.concatenate([qn[:MLA_NOPE], r1, r2, zpad_mla], axis=0) * q_scale).astype(BF16)
        base = hd * (MLA_NOPE + MLA_V)
        kn = kvT[base:base + MLA_NOPE]
        r = lax.rsqrt((jnp.sum(kn * kn, axis=0, keepdims=True) + kpe_ss) * (1.0 / MLA_QK) + EPS)
        kblk = jnp.concatenate([kn * r * gk[:MLA_NOPE], kr1 * r, kr2 * r, zpad_mla], axis=0)
        k_ref[0, hd] = kblk.T.astype(BF16)
        vt_ref[0, hd, 0, :MLA_V] = kvT[base + MLA_NOPE:base + MLA_NOPE + MLA_V].astype(BF16)
        vt_ref[0, hd, 0, MLA_V:] = ones_rows

    z = projT[OFF_FL:OFF_FL + FOX_HEADS] + bf_ref[...]
    logf = -(jnp.maximum(-z, 0.0) + jnp.log1p(jnp.exp(-jnp.abs(z))))
    lhi, lmid, llo = _split3(logf)
    tri = (lax.broadcasted_iota(jnp.int32, (tm, tm), 0) <= lax.broadcasted_iota(jnp.int32, (tm, tm), 1))
    tri = jnp.where(tri, 1.0, 0.0).astype(BF16)
    parts = jnp.dot(jnp.concatenate([lhi, lmid, llo], axis=0).astype(BF16), tri, preferred_element_type=F32)
    fcum = (parts[2 * FOX_HEADS:] + parts[FOX_HEADS:2 * FOX_HEADS]) + parts[:FOX_HEADS] + carry_ref[:, 0:1]
    carry_ref[...] = jnp.broadcast_to(fcum[:, tm - 1:tm], carry_ref.shape)
    fc = fcum * LOG2E

    gfq, gfk = gfq_ref[...], gfk_ref[...]
    zpad_fox = jnp.zeros((HEAD_PAD - FOX_DIM - AUG_ROWS, tm), F32)
    f_scale = FOX_DIM ** -0.5 * LOG2E
    for hd in range(FOX_HEADS):
        fq = projT[OFF_FQ + hd * FOX_DIM:OFF_FQ + (hd + 1) * FOX_DIM]
        fk = projT[OFF_FK + hd * FOX_DIM:OFF_FK + (hd + 1) * FOX_DIM]
        fv = projT[OFF_FV + hd * FOX_DIM:OFF_FV + (hd + 1) * FOX_DIM]
        f_row = fc[hd:hd + 1]
        qn = fq * rms_rows(fq, FOX_DIM) * gfq * f_scale
        qblk = jnp.concatenate([qn, _aug_block(_split3(f_row), False, tm), zpad_fox], axis=0)
        qt_ref[0, MLA_HEADS + hd] = qblk.astype(BF16)
        kn = fk * rms_rows(fk, FOX_DIM) * gfk
        kblk = jnp.concatenate([kn, _aug_block(_split3(-f_row), True, tm), zpad_fox], axis=0)
        k_ref[0, MLA_HEADS + hd] = kblk.T.astype(BF16)
        vt_ref[0, MLA_HEADS + hd, 0, :MLA_V] = fv.astype(BF16)
        vt_ref[0, MLA_HEADS + hd, 0, MLA_V:] = ones_rows


def _prep(x, positions, mod, norm1_g, w_in, mla_qa_g, mla_kva_g, w_uq, w_ukv, mla_q_g, mla_k_g,
          fox_q_g, fox_k_g, b_f):
    b, s, d = x.shape
    tm = PREP_TOKENS
    nj = s // tm
    col = lambda v: v.astype(F32).reshape(-1, 1)
    inv_freq = ROPE_THETA ** (-jnp.arange(0, MLA_ROPE, 2, dtype=F32) / MLA_ROPE)
    in_width = w_in.shape[1]
    args = (x, positions.reshape(b, 1, s), mod, norm1_g.astype(F32).reshape(1, d), w_in.T.astype(BF16),
            col(mla_qa_g), col(mla_kva_g), w_uq.T.astype(BF16), w_ukv.T.astype(BF16),
            col(mla_q_g), col(mla_k_g), col(fox_q_g), col(fox_k_g), col(b_f), col(inv_freq))
    in_specs = [
        pl.BlockSpec((1, tm, d), lambda i, j: (i, j, 0)),
        pl.BlockSpec((1, 1, tm), lambda i, j: (i, 0, j)),
        pl.BlockSpec((1, 6, d), lambda i, j: (i, 0, 0)),
    ] + [_full(a.shape) for a in args[3:]]
    assert in_width == OFF_FL + FOX_HEADS
    return pl.pallas_call(
        _prep_kernel,
        grid=(b, nj),
        in_specs=in_specs,
        out_specs=[
            pl.BlockSpec((1, HEADS, HEAD_PAD, tm), lambda i, j: (i, 0, 0, j)),
            pl.BlockSpec((1, HEADS, tm, HEAD_PAD), lambda i, j: (i, 0, j, 0)),
            pl.BlockSpec((1, HEADS, 1, V_ROWS, tm), lambda i, j: (i, 0, j, 0, 0)),
        ],
        out_shape=[
            jax.ShapeDtypeStruct((b, HEADS, HEAD_PAD, s), BF16),
            jax.ShapeDtypeStruct((b, HEADS, s, HEAD_PAD), BF16),
            jax.ShapeDtypeStruct((b, HEADS, nj, V_ROWS, tm), BF16),
        ],
        scratch_shapes=[pltpu.VMEM((FOX_HEADS, LANES), F32)],
        compiler_params=_params("arbitrary", "arbitrary"),
        name="token_prep",
    )(*args)


def _attn_kernel(qt_ref, k_ref, vt_ref, o_ref, s_ref):
    blk = qt_ref.shape[3]
    i = pl.program_id(2)
    qt = qt_ref[0, 0]

    def scores(slot, j):
        kb = k_ref[0, 0, pl.ds(pl.multiple_of(j * blk, blk), blk), :]
        s_ref[slot] = jnp.dot(kb, qt, preferred_element_type=F32)

    def consume(slot, j, carry, diagonal):
        m, acc = carry
        s = s_ref[slot]
        if diagonal:
            keep = lax.broadcasted_iota(jnp.int32, (blk, blk), 0) <= lax.broadcasted_iota(jnp.int32, (blk, blk), 1)
            s = jnp.where(keep, s, -jnp.inf)
        m_new = jnp.maximum(m, jnp.max(s, axis=0, keepdims=True))
        p = jnp.exp2(s - m_new).astype(BF16)
        acc = jnp.exp2(m - m_new) * acc + jnp.dot(vt_ref[0, 0, j], p, preferred_element_type=F32)
        return m_new, acc

    def finish(carry):
        _, acc = carry
        o_ref[0] = (acc[:MLA_V] / acc[MLA_V:MLA_V + 1]).astype(o_ref.dtype)

    def pair(jj, carry):
        j0 = 2 * jj
        scores(1, j0 + 1)
        carry = consume(0, j0, carry, False)
        scores(0, j0 + 2)
        return consume(1, j0 + 1, carry, False)

    init = (jnp.full((1, blk), -jnp.inf, F32), jnp.zeros((vt_ref.shape[3], blk), F32))
    scores(0, 0)
    carry = lax.fori_loop(0, i // 2, pair, init)

    @pl.when(i % 2 == 0)
    def _():
        finish(consume(0, i, carry, True))

    @pl.when(i % 2 == 1)
    def _():
        scores(1, i)
        finish(consume(1, i, consume(0, i - 1, carry, False), True))


def _attention(qt, k, vt):
    b, heads, _, s = qt.shape
    blk = ATTN_BLOCK
    assert vt.shape[4] == blk
    return pl.pallas_call(
        _attn_kernel,
        grid=(b, heads, s // blk),
        in_specs=[
            pl.BlockSpec((1, 1, HEAD_PAD, blk), lambda bi, h, i: (bi, h, 0, i)),
            pl.BlockSpec((1, 1, s, HEAD_PAD), lambda bi, h, i: (bi, h, 0, 0)),
            pl.BlockSpec((1, 1, s // blk, V_ROWS, blk), lambda bi, h, i: (bi, h, 0, 0, 0)),
        ],
        out_specs=pl.BlockSpec((1, MLA_V, blk), lambda bi, h, i: (bi, h, i)),
        out_shape=jax.ShapeDtypeStruct((b, heads * MLA_V, s), BF16),
        scratch_shapes=[pltpu.VMEM((2, blk, blk), F32)],
        compiler_params=_params("arbitrary", "arbitrary", "arbitrary"),
        name="flash_attention",
    )(qt, k, vt)


def _proj_kernel(ot_ref, x_ref, mod_ref, wo_ref, g2_ref, x1_ref, h2_ref):
    attn = lax.dot_general(ot_ref[0], wo_ref[...], TN_DIMS, preferred_element_type=F32)
    x1 = x_ref[0] + mod_ref[0, 2:3, :] * attn
    x1_ref[0] = x1
    y = x1 * lax.rsqrt(jnp.mean(x1 * x1, axis=-1, keepdims=True) + EPS) * g2_ref[...]
    h2_ref[0] = (y * (1.0 + mod_ref[0, 4:5, :]) + mod_ref[0, 3:4, :]).astype(h2_ref.dtype)


def _out_proj(ot, x, mod, w_o, norm2_g):
    b, s, d = x.shape
    tm = PROJ_TOKENS
    width = ot.shape[1]
    return pl.pallas_call(
        _proj_kernel,
        grid=(b, s // tm),
        in_specs=[
            pl.BlockSpec((1, width, tm), lambda i, j: (i, 0, j)),
            pl.BlockSpec((1, tm, d), lambda i, j: (i, j, 0)),
            pl.BlockSpec((1, 6, d), lambda i, j: (i, 0, 0)),
            _full((width, d)),
            _full((1, d)),
        ],
        out_specs=[pl.BlockSpec((1, tm, d), lambda i, j: (i, j, 0))] * 2,
        out_shape=[jax.ShapeDtypeStruct((b, s, d), F32), jax.ShapeDtypeStruct((b, s, d), BF16)],
        compiler_params=_params("arbitrary", "arbitrary"),
        name="out_proj",
    )(ot, x, mod, w_o.astype(BF16), norm2_g.astype(F32).reshape(1, d))


def _top_values(s, want_rank):
    work = s
    rank = jnp.full(s.shape, float(PEER_TOPK), F32) if want_rank else None
    vals = []
    for i in range(PEER_TOPK):
        m = jnp.max(work, axis=0, keepdims=True)
        hit = work == m
        if want_rank:
            rank = jnp.where(hit, float(i), rank)
        work = jnp.where(hit, -jnp.inf, work)
        vals.append(m)
    return jnp.concatenate(vals, axis=0), rank


def _route(s1, s2):
    v1, _ = _top_values(s1, False)
    v2, rank2 = _top_values(s2, True)
    cands = [v1[0:1] + v2]
    cands += [v1[i:i + 1] + v2[0:8] for i in range(1, 8)]
    cands += [v1[8:16] + v2[0:1]]
    cand = jnp.concatenate(cands, axis=0)
    top = v1[0:1] + v2[0:1]
    zsum = jnp.zeros_like(top)
    tau = top
    for _ in range(PEER_TOPK):
        tau = jnp.max(cand, axis=0, keepdims=True)
        zsum = zsum + jnp.exp(tau - top)
        cand = jnp.where(cand == tau, -jnp.inf, cand)
    count = jnp.zeros(s1.shape, F32)
    for j in range(PEER_TOPK):
        count = count + jnp.where(s1 + v2[j:j + 1] >= tau, 1.0, 0.0)
    pa_half = jnp.exp(s1 - v1[0:1]) * (0.5 / zsum)
    pb = jnp.exp(s2 - v2[0:1])
    return rank2, count, pa_half, pb


GELU_C1 = math.sqrt(2.0 / math.pi)
GELU_C2 = GELU_C1 * 0.044715


def _gelu_times_two(a):
    t = jnp.tanh(a * (GELU_C1 + GELU_C2 * (a * a)))
    return a + a * t


def _peer_kernel(h2_ref, x1_ref, mod_ref, wpqT_ref, keys_ref, u_ref, vT_ref, o_ref,
                 rank_ref, pb_ref, cnt_ref, pa_ref, sc_ref, acc_ref):
    j = pl.program_id(1)
    tb = h2_ref.shape[0]
    subs = u_ref.shape[0] // PEER_SUB
    chunks = tb // LANES
    tiles = N_KEYS // BF16_ROWS
    h2 = h2_ref[...]

    @pl.when(j == 0)
    def _():
        acc_ref[...] = jnp.zeros_like(acc_ref)

        def head_body(hd, carry):
            w_rows = wpqT_ref[pl.ds(pl.multiple_of(hd * 2 * PEER_HALF, 2 * PEER_HALF), 2 * PEER_HALF), :]
            qpT = lax.dot_general(w_rows, h2, NT_DIMS, preferred_element_type=F32)
            for half in range(2):
                sc = jnp.dot(keys_ref[hd, half], qpT[half * PEER_HALF:(half + 1) * PEER_HALF],
                             preferred_element_type=F32, precision=lax.Precision.HIGHEST)
                for c in range(chunks):
                    sc_ref[half, c] = sc[:, c * LANES:(c + 1) * LANES]

            def chunk_body(c, carry2):
                rank2, count, pa_half, pb = _route(sc_ref[0, c], sc_ref[1, c])
                rank_ref[hd, c] = rank2.astype(BF16).reshape(tiles, BF16_ROWS, LANES)
                pb_ref[hd, c] = pb.astype(BF16).reshape(tiles, BF16_ROWS, LANES)
                cnt_ref[hd, c] = count
                pa_ref[hd, c] = pa_half
                return carry2

            return lax.fori_loop(0, chunks, chunk_body, carry)

        lax.fori_loop(0, PEER_HEADS, head_body, 0)

    rows = PEER_SUB // N_KEYS
    zero = jnp.zeros((), BF16)

    def first_matmul(sb):
        return lax.dot_general(u_ref[sb * PEER_SUB:(sb + 1) * PEER_SUB, :], h2, NT_DIMS,
                               preferred_element_type=F32)

    a_next = first_matmul(0)
    for sb in range(subs):
        aT = a_next
        if sb + 1 < subs:
            a_next = first_matmul(sb + 1)
        e1 = [(j * subs + sb) * rows + r for r in range(rows)]
        w_cols = [[] for _ in range(rows)]
        for c in range(chunks):
            w = [jnp.zeros((tiles, BF16_ROWS, LANES), BF16) for _ in range(rows)]
            for hd in range(PEER_HEADS):
                rk, pb = rank_ref[hd, c], pb_ref[hd, c]
                for r in range(rows):
                    cnt = jnp.broadcast_to(cnt_ref[hd, c, pl.ds(e1[r], 1), :], (BF16_ROWS, LANES)).astype(BF16)
                    pa = jnp.broadcast_to(pa_ref[hd, c, pl.ds(e1[r], 1), :], (BF16_ROWS, LANES)).astype(BF16)
                    w[r] = w[r] + jnp.where(rk < cnt[None], pb, zero) * pa[None]
            for r in range(rows):
                w_cols[r].append(w[r].reshape(N_KEYS, LANES))
        zs = []
        for r in range(rows):
            g = _gelu_times_two(aT[r * N_KEYS:(r + 1) * N_KEYS]).astype(BF16)
            zs.append(g * jnp.concatenate(w_cols[r], axis=1))
        zT = jnp.concatenate(zs, axis=0)
        acc_ref[...] += jnp.dot(vT_ref[:, sb * PEER_SUB:(sb + 1) * PEER_SUB], zT,
                                preferred_element_type=F32)

    @pl.when(j == pl.num_programs(1) - 1)
    def _():
        o_ref[...] = x1_ref[...] + mod_ref[0, 5:6, :] * acc_ref[...].T


def _peer(h2, x1, mod, w_pq, sub_keys, peer_u, peer_v, seq):
    t, d = h2.shape
    tb, eb = PEER_TOKENS, PEER_EXPERTS
    n_exp = peer_u.shape[0]
    assert eb % PEER_SUB == 0 and PEER_SUB % N_KEYS == 0
    chunks = tb // LANES
    packed = pltpu.VMEM((PEER_HEADS, chunks, N_KEYS // BF16_ROWS, BF16_ROWS, LANES), BF16)
    rowtab = pltpu.VMEM((PEER_HEADS, chunks, N_KEYS, LANES), F32)
    scores = pltpu.VMEM((2, chunks, N_KEYS, LANES), F32)
    return pl.pallas_call(
        _peer_kernel,
        grid=(t // tb, n_exp // eb),
        in_specs=[
            pl.BlockSpec((tb, d), lambda i, j: (i, 0)),
            pl.BlockSpec((tb, d), lambda i, j: (i, 0)),
            pl.BlockSpec((1, 6, d), lambda i, j: (i * tb // seq, 0, 0)),
            _full((w_pq.shape[1], d)),
            _full(sub_keys.shape),
            pl.BlockSpec((eb, d), lambda i, j: (j, 0)),
            pl.BlockSpec((d, eb), lambda i, j: (0, j)),
        ],
        out_specs=pl.BlockSpec((tb, d), lambda i, j: (i, 0)),
        out_shape=jax.ShapeDtypeStruct((t, d), F32),
        scratch_shapes=[packed, packed, rowtab, rowtab, scores, pltpu.VMEM((d, tb), F32)],
        compiler_params=_params("arbitrary", "arbitrary"),
        name="peer",
    )(h2, x1, mod, w_pq.T.astype(BF16), sub_keys.astype(F32), peer_u.astype(BF16), peer_v.T.astype(BF16))


def kernel(x, c, positions, w_ada, b_ada, norm1_g, w_in, mla_qa_g, mla_kva_g, w_uq, w_ukv, mla_q_g, mla_k_g,
           fox_q_g, fox_k_g, b_f, w_o, norm2_g, w_pq, sub_keys, peer_u, peer_v):
    b, s, d = x.shape
    assert s % PREP_TOKENS == 0 and s % ATTN_BLOCK == 0 and s % PROJ_TOKENS == 0 and s % PEER_TOKENS == 0
    assert PREP_TOKENS == ATTN_BLOCK and peer_u.shape[0] == N_KEYS * N_KEYS
    mod = _ada(c, w_ada, b_ada)
    qt, k, vt = _prep(x, positions, mod, norm1_g, w_in, mla_qa_g, mla_kva_g, w_uq, w_ukv, mla_q_g, mla_k_g,
                      fox_q_g, fox_k_g, b_f)
    ot = _attention(qt, k, vt)
    x1, h2 = _out_proj(ot, x, mod, w_o, norm2_g)
    out = _peer(h2.reshape(b * s, d), x1.reshape(b * s, d), mod, w_pq, sub_keys, peer_u, peer_v, s)
    return out.reshape(b, s, d).astype(x.dtype)
```

```python
import functools
import math

import jax
import jax.numpy as jnp
from jax import lax
from jax.experimental import pallas as pl
from jax.experimental.pallas import tpu as pltpu

F32 = jnp.float32
BF16 = jnp.bfloat16

EPS = 1e-6
LOG2E = 1.4426950408889634
ROPE_THETA = 10000.0

LANES = 128
VMEM_LIMIT_BYTES = 56 * 1024 * 1024

MLA_HEADS = 8
MLA_Q_RANK = 384
MLA_KV_RANK = 256
MLA_NOPE = 64
MLA_ROPE = 32
MLA_QK = MLA_NOPE + MLA_ROPE
MLA_V = 64
FOX_HEADS = 8
FOX_DIM = 64
FOX_WIDTH = FOX_HEADS * FOX_DIM
HEADS = MLA_HEADS + FOX_HEADS
HEAD_PAD = 128
AUG_ROWS = 16
V_ROWS = 80
PEER_HEADS = 8
N_KEYS = 128
PEER_HALF = 128
PEER_TOPK = 16
OFF_CQ = 0
OFF_CKV = OFF_CQ + MLA_Q_RANK
OFF_KPE = OFF_CKV + MLA_KV_RANK
OFF_FQ = OFF_KPE + MLA_ROPE
OFF_FK = OFF_FQ + FOX_WIDTH
OFF_FV = OFF_FK + FOX_WIDTH
OFF_FL = OFF_FV + FOX_WIDTH

PREP_TOKENS = 512
ATTN_BLOCK = 512
PROJ_TOKENS = 512
PEER_TOKENS = 512
PEER_EXPERTS = 2048
PEER_SUB = 512
PEER_LOOKAHEAD = 2
BF16_ROWS = 16

NT_DIMS = (((1,), (1,)), ((), ()))
TN_DIMS = (((0,), (0,)), ((), ()))


def _params(*semantics):
    return pltpu.CompilerParams(dimension_semantics=semantics, vmem_limit_bytes=VMEM_LIMIT_BYTES)


def _full(shape):
    n = len(shape)
    return pl.BlockSpec(shape, lambda *_: (0,) * n)


def _ada_kernel(c_ref, w_ref, b_ref, o_ref):
    c = c_ref[...]
    s = c / (1.0 + jnp.exp(-c))
    o_ref[...] = jnp.dot(s, w_ref[...], preferred_element_type=F32,
                         precision=lax.Precision.HIGHEST) + b_ref[...]


def _ada(c, w_ada, b_ada):
    b, d = c.shape
    n = w_ada.shape[1]
    rows = 8
    cp = jnp.zeros((rows, d), F32).at[:b].set(c.astype(F32))
    out = pl.pallas_call(
        _ada_kernel,
        grid=(n // d,),
        in_specs=[_full((rows, d)), pl.BlockSpec((d, d), lambda j: (0, j)), pl.BlockSpec((1, d), lambda j: (0, j))],
        out_specs=pl.BlockSpec((rows, d), lambda j: (0, j)),
        out_shape=jax.ShapeDtypeStruct((rows, n), F32),
        compiler_params=_params("arbitrary"),
        name="adaln",
    )(cp, w_ada.astype(F32), b_ada.astype(F32).reshape(1, n))
    return out[:b].reshape(b, n // d, d)


def _split3(f):
    hi = f.astype(BF16).astype(F32)
    r = f - hi
    mid = r.astype(BF16).astype(F32)
    return hi, mid, r - mid


def _aug_block(vals, ones_first, width):
    row = lax.broadcasted_iota(jnp.int32, (AUG_ROWS, width), 0)
    v0, v1, v2 = vals
    off = 3 if ones_first else 0
    blk = jnp.where(row == off, v0, jnp.where(row == off + 1, v1, jnp.where(row == off + 2, v2, 0.0)))
    one_lo = 0 if ones_first else 3
    return jnp.where((row >= one_lo) & (row < one_lo + 3), 1.0, blk)


def _prep_kernel(x_ref, pos_ref, mod_ref, g1_ref, winT_ref, gqa_ref, gkva_ref, wuqT_ref, wukvT_ref,
                 gq_ref, gk_ref, gfq_ref, gfk_ref, bf_ref, invf_ref,
                 qt_ref, k_ref, vt_ref, carry_ref):
    tm = x_ref.shape[1]

    @pl.when(pl.program_id(1) == 0)
    def _():
        carry_ref[...] = jnp.zeros_like(carry_ref)

    x = x_ref[0]
    y = x * lax.rsqrt(jnp.mean(x * x, axis=-1, keepdims=True) + EPS) * g1_ref[...]
    h = y * (1.0 + mod_ref[0, 1:2, :]) + mod_ref[0, 0:1, :]
    projT = lax.dot_general(winT_ref[...], h.astype(BF16), NT_DIMS, preferred_element_type=F32)

    def rms_rows(v, n):
        return lax.rsqrt(jnp.sum(v * v, axis=0, keepdims=True) * (1.0 / n) + EPS)

    cq = projT[OFF_CQ:OFF_CKV]
    cqn = (cq * rms_rows(cq, MLA_Q_RANK) * gqa_ref[...]).astype(BF16)
    qaT = jnp.dot(wuqT_ref[...], cqn, preferred_element_type=F32)
    ckv = projT[OFF_CKV:OFF_KPE]
    ckvn = (ckv * rms_rows(ckv, MLA_KV_RANK) * gkva_ref[...]).astype(BF16)
    kvT = jnp.dot(wukvT_ref[...], ckvn, preferred_element_type=F32)
    kpe = projT[OFF_KPE:OFF_FQ]

    ang = pos_ref[0].astype(F32) * invf_ref[...]
    cos, sin = jnp.cos(ang), jnp.sin(ang)
    half = MLA_ROPE // 2

    def rope(v):
        v1, v2 = v[:half], v[half:]
        return v1 * cos - v2 * sin, v2 * cos + v1 * sin

    gq, gk = gq_ref[...], gk_ref[...]
    kpe_ss = jnp.sum(kpe * kpe, axis=0, keepdims=True)
    kr1, kr2 = rope(kpe * gk[MLA_NOPE:])
    zpad_mla = jnp.zeros((HEAD_PAD - MLA_QK, tm), F32)
    ones_rows = jnp.ones((V_ROWS - MLA_V, tm), BF16)
    q_scale = MLA_QK ** -0.5 * LOG2E
    for hd in range(MLA_HEADS):
        q = qaT[hd * MLA_QK:(hd + 1) * MLA_QK]
        qn = q * rms_rows(q, MLA_QK) * gq
        r1, r2 = rope(qn[MLA_NOPE:])
        qt_ref[0, hd] = (jnp.concatenate([qn[:MLA_NOPE], r1, r2, zpad_mla], axis=0) * q_scale).astype(BF16)
        base = hd * (MLA_NOPE + MLA_V)
        kn = kvT[base:base + MLA_NOPE]
        r = lax.rsqrt((jnp.sum(kn * kn, axis=0, keepdims=True) + kpe_ss) * (1.0 / MLA_QK) + EPS)
        kblk = jnp.concatenate([kn * r * gk[:MLA_NOPE], kr1 * r, kr2 * r, zpad_mla], axis=0)
        k_ref[0, hd] = kblk.T.astype(BF16)
        vt_ref[0, hd, 0, :MLA_V] = kvT[base + MLA_NOPE:base + MLA_NOPE + MLA_V].astype(BF16)
        vt_ref[0, hd, 0, MLA_V:] = ones_rows

    z = projT[OFF_FL:OFF_FL + FOX_HEADS] + bf_ref[...]
    logf = -(jnp.maximum(-z, 0.0) + jnp.log1p(jnp.exp(-jnp.abs(z))))
    lhi, lmid, llo = _split3(logf)
    tri = (lax.broadcasted_iota(jnp.int32, (tm, tm), 0) <= lax.broadcasted_iota(jnp.int32, (tm, tm), 1))
    tri = jnp.where(tri, 1.0, 0.0).astype(BF16)
    parts = jnp.dot(jnp.concatenate([lhi, lmid, llo], axis=0).astype(BF16), tri, preferred_element_type=F32)
    fcum = (parts[2 * FOX_HEADS:] + parts[FOX_HEADS:2 * FOX_HEADS]) + parts[:FOX_HEADS] + carry_ref[:, 0:1]
    carry_ref[...] = jnp.broadcast_to(fcum[:, tm - 1:tm], carry_ref.shape)
    fc = fcum * LOG2E

    gfq, gfk = gfq_ref[...], gfk_ref[...]
    zpad_fox = jnp.zeros((HEAD_PAD - FOX_DIM - AUG_ROWS, tm), F32)
    f_scale = FOX_DIM ** -0.5 * LOG2E
    for hd in range(FOX_HEADS):
        fq = projT[OFF_FQ + hd * FOX_DIM:OFF_FQ + (hd + 1) * FOX_DIM]
        fk = projT[OFF_FK + hd * FOX_DIM:OFF_FK + (hd + 1) * FOX_DIM]
        fv = projT[OFF_FV + hd * FOX_DIM:OFF_FV + (hd + 1) * FOX_DIM]
        f_row = fc[hd:hd + 1]
        qn = fq * rms_rows(fq, FOX_DIM) * gfq * f_scale
        qblk = jnp.concatenate([qn, _aug_block(_split3(f_row), False, tm), zpad_fox], axis=0)
        qt_ref[0, MLA_HEADS + hd] = qblk.astype(BF16)
        kn = fk * rms_rows(fk, FOX_DIM) * gfk
        kblk = jnp.concatenate([kn, _aug_block(_split3(-f_row), True, tm), zpad_fox], axis=0)
        k_ref[0, MLA_HEADS + hd] = kblk.T.astype(BF16)
        vt_ref[0, MLA_HEADS + hd, 0, :MLA_V] = fv.astype(BF16)
        vt_ref[0, MLA_HEADS + hd, 0, MLA_V:] = ones_rows


def _prep(x, positions, mod, norm1_g, w_in, mla_qa_g, mla_kva_g, w_uq, w_ukv, mla_q_g, mla_k_g,
          fox_q_g, fox_k_g, b_f):
    b, s, d = x.shape
    tm = PREP_TOKENS
    nj = s // tm
    col = lambda v: v.astype(F32).reshape(-1, 1)
    inv_freq = ROPE_THETA ** (-jnp.arange(0, MLA_ROPE, 2, dtype=F32) / MLA_ROPE)
    in_width = w_in.shape[1]
    args = (x, positions.reshape(b, 1, s), mod, norm1_g.astype(F32).reshape(1, d), w_in.T.astype(BF16),
            col(mla_qa_g), col(mla_kva_g), w_uq.T.astype(BF16), w_ukv.T.astype(BF16),
            col(mla_q_g), col(mla_k_g), col(fox_q_g), col(fox_k_g), col(b_f), col(inv_freq))
    in_specs = [
        pl.BlockSpec((1, tm, d), lambda i, j: (i, j, 0)),
        pl.BlockSpec((1, 1, tm), lambda i, j: (i, 0, j)),
        pl.BlockSpec((1, 6, d), lambda i, j: (i, 0, 0)),
    ] + [_full(a.shape) for a in args[3:]]
    assert in_width == OFF_FL + FOX_HEADS
    return pl.pallas_call(
        _prep_kernel,
        grid=(b, nj),
        in_specs=in_specs,
        out_specs=[
            pl.BlockSpec((1, HEADS, HEAD_PAD, tm), lambda i, j: (i, 0, 0, j)),
            pl.BlockSpec((1, HEADS, tm, HEAD_PAD), lambda i, j: (i, 0, j, 0)),
            pl.BlockSpec((1, HEADS, 1, V_ROWS, tm), lambda i, j: (i, 0, j, 0, 0)),
        ],
        out_shape=[
            jax.ShapeDtypeStruct((b, HEADS, HEAD_PAD, s), BF16),
            jax.ShapeDtypeStruct((b, HEADS, s, HEAD_PAD), BF16),
            jax.ShapeDtypeStruct((b, HEADS, nj, V_ROWS, tm), BF16),
        ],
        scratch_shapes=[pltpu.VMEM((FOX_HEADS, LANES), F32)],
        compiler_params=_params("arbitrary", "arbitrary"),
        name="token_prep",
    )(*args)


def _attn_kernel(qt_ref, k_ref, vt_ref, o_ref, s_ref):
    blk = qt_ref.shape[3]
    i = pl.program_id(2)
    qt = qt_ref[0, 0]

    def scores(slot, j):
        kb = k_ref[0, 0, pl.ds(pl.multiple_of(j * blk, blk), blk), :]
        s_ref[slot] = jnp.dot(kb, qt, preferred_element_type=F32)

    def consume(slot, j, carry, diagonal):
        m, acc = carry
        s = s_ref[slot]
        if diagonal:
            keep = lax.broadcasted_iota(jnp.int32, (blk, blk), 0) <= lax.broadcasted_iota(jnp.int32, (blk, blk), 1)
            s = jnp.where(keep, s, -jnp.inf)
        m_new = jnp.maximum(m, jnp.max(s, axis=0, keepdims=True))
        p = jnp.exp2(s - m_new).astype(BF16)
        acc = jnp.exp2(m - m_new) * acc + jnp.dot(vt_ref[0, 0, j], p, preferred_element_type=F32)
        return m_new, acc

    def finish(carry):
        _, acc = carry
        o_ref[0] = (acc[:MLA_V] / acc[MLA_V:MLA_V + 1]).astype(o_ref.dtype)

    def pair(jj, carry):
        j0 = 2 * jj
        scores(1, j0 + 1)
        carry = consume(0, j0, carry, False)
        scores(0, j0 + 2)
        return consume(1, j0 + 1, carry, False)

    init = (jnp.full((1, blk), -jnp.inf, F32), jnp.zeros((vt_ref.shape[3], blk), F32))
    scores(0, 0)
    carry = lax.fori_loop(0, i // 2, pair, init)

    @pl.when(i % 2 == 0)
    def _():
        finish(consume(0, i, carry, True))

    @pl.when(i % 2 == 1)
    def _():
        scores(1, i)
        finish(consume(1, i, consume(0, i - 1, carry, False), True))


def _attention(qt, k, vt):
    b, heads, _, s = qt.shape
    blk = ATTN_BLOCK
    assert vt.shape[4] == blk
    return pl.pallas_call(
        _attn_kernel,
        grid=(b, heads, s // blk),
        in_specs=[
            pl.BlockSpec((1, 1, HEAD_PAD, blk), lambda bi, h, i: (bi, h, 0, i)),
            pl.BlockSpec((1, 1, s, HEAD_PAD), lambda bi, h, i: (bi, h, 0, 0)),
            pl.BlockSpec((1, 1, s // blk, V_ROWS, blk), lambda bi, h, i: (bi, h, 0, 0, 0)),
        ],
        out_specs=pl.BlockSpec((1, MLA_V, blk), lambda bi, h, i: (bi, h, i)),
        out_shape=jax.ShapeDtypeStruct((b, heads * MLA_V, s), BF16),
        scratch_shapes=[pltpu.VMEM((2, blk, blk), F32)],
        compiler_params=_params("arbitrary", "arbitrary", "arbitrary"),
        name="flash_attention",
    )(qt, k, vt)


def _proj_kernel(ot_ref, x_ref, mod_ref, wo_ref, g2_ref, x1_ref, h2_ref):
    attn = lax.dot_general(ot_ref[0], wo_ref[...], TN_DIMS, preferred_element_type=F32)
    x1 = x_ref[0] + mod_ref[0, 2:3, :] * attn
    x1_ref[0] = x1
    y = x1 * lax.rsqrt(jnp.mean(x1 * x1, axis=-1, keepdims=True) + EPS) * g2_ref[...]
    h2_ref[0] = (y * (1.0 + mod_ref[0, 4:5, :]) + mod_ref[0, 3:4, :]).astype(h2_ref.dtype)


def _out_proj(ot, x, mod, w_o, norm2_g):
    b, s, d = x.shape
    tm = PROJ_TOKENS
    width = ot.shape[1]
    return pl.pallas_call(
        _proj_kernel,
        grid=(b, s // tm),
        in_specs=[
            pl.BlockSpec((1, width, tm), lambda i, j: (i, 0, j)),
            pl.BlockSpec((1, tm, d), lambda i, j: (i, j, 0)),
            pl.BlockSpec((1, 6, d), lambda i, j: (i, 0, 0)),
            _full((width, d)),
            _full((1, d)),
        ],
        out_specs=[pl.BlockSpec((1, tm, d), lambda i, j: (i, j, 0))] * 2,
        out_shape=[jax.ShapeDtypeStruct((b, s, d), F32), jax.ShapeDtypeStruct((b, s, d), BF16)],
        compiler_params=_params("arbitrary", "arbitrary"),
        name="out_proj",
    )(ot, x, mod, w_o.astype(BF16), norm2_g.astype(F32).reshape(1, d))


def _top_values(s, want_rank):
    work = s
    rank = jnp.full(s.shape, float(PEER_TOPK), F32) if want_rank else None
    vals = []
    for i in range(PEER_TOPK):
        m = jnp.max(work, axis=0, keepdims=True)
        hit = work == m
        if want_rank:
            rank = jnp.where(hit, float(i), rank)
        work = jnp.where(hit, -jnp.inf, work)
        vals.append(m)
    return jnp.concatenate(vals, axis=0), rank


def _route(s1, s2):
    v1, _ = _top_values(s1, False)
    v2, rank2 = _top_values(s2, True)
    cands = [v1[0:1] + v2]
    cands += [v1[i:i + 1] + v2[0:8] for i in range(1, 8)]
    cands += [v1[8:16] + v2[0:1]]
    cand = jnp.concatenate(cands, axis=0)
    top = v1[0:1] + v2[0:1]
    zsum = jnp.zeros_like(top)
    tau = top
    for _ in range(PEER_TOPK):
        tau = jnp.max(cand, axis=0, keepdims=True)
        zsum = zsum + jnp.exp(tau - top)
        cand = jnp.where(cand == tau, -jnp.inf, cand)
    count = jnp.zeros(s1.shape, F32)
    for j in range(PEER_TOPK):
        count = count + jnp.where(s1 + v2[j:j + 1] >= tau, 1.0, 0.0)
    pa_half = jnp.exp(s1 - v1[0:1]) * (0.5 / zsum)
    pb = jnp.exp(s2 - v2[0:1])
    return rank2, count, pa_half, pb


GELU_C1 = math.sqrt(2.0 / math.pi)
GELU_C2 = GELU_C1 * 0.044715


def _gelu_times_two(a):
    t = jnp.tanh(a * (jnp.asarray(GELU_C1, a.dtype) + jnp.asarray(GELU_C2, a.dtype) * (a * a)))
    return a + a * t


def _peer_kernel(h2_ref, x1_ref, mod_ref, wpqT_ref, khi_ref, klo_ref, u_ref, vT_ref, o_ref,
                 rank_ref, pb_ref, cnt_ref, pa_ref, sc_ref, acc_ref):
    j = pl.program_id(1)
    tb = h2_ref.shape[0]
    subs = u_ref.shape[0] // PEER_SUB
    chunks = tb // LANES
    tiles = N_KEYS // BF16_ROWS
    h2 = h2_ref[...]

    @pl.when(j == 0)
    def _():
        acc_ref[...] = jnp.zeros_like(acc_ref)

        def head_body(hd, carry):
            w_rows = wpqT_ref[pl.ds(pl.multiple_of(hd * 2 * PEER_HALF, 2 * PEER_HALF), 2 * PEER_HALF), :]
            qpT = lax.dot_general(w_rows, h2, NT_DIMS, preferred_element_type=F32)
            q_hi = qpT.astype(BF16)
            q_lo = (qpT - q_hi.astype(F32)).astype(BF16)
            for half in range(2):
                rows_q = slice(half * PEER_HALF, (half + 1) * PEER_HALF)
                k_hi, k_lo = khi_ref[hd, half], klo_ref[hd, half]
                sc = jnp.dot(k_hi, q_hi[rows_q], preferred_element_type=F32) + (
                    jnp.dot(k_hi, q_lo[rows_q], preferred_element_type=F32)
                    + jnp.dot(k_lo, q_hi[rows_q], preferred_element_type=F32))
                for c in range(chunks):
                    sc_ref[half, c] = sc[:, c * LANES:(c + 1) * LANES]

            def chunk_body(c, carry2):
                rank2, count, pa_half, pb = _route(sc_ref[0, c], sc_ref[1, c])
                rank_ref[hd, c] = rank2.astype(BF16)
                pb_ref[hd, c] = pb.astype(BF16)
                cnt_ref[hd, c] = count
                pa_ref[hd, c] = pa_half
                return carry2

            return lax.fori_loop(0, chunks, chunk_body, carry)

        lax.fori_loop(0, PEER_HEADS, head_body, 0)

    rows = PEER_SUB // N_KEYS
    zero = jnp.zeros((), BF16)

    def first_matmul(sb):
        return lax.dot_general(u_ref[sb * PEER_SUB:(sb + 1) * PEER_SUB, :], h2, NT_DIMS,
                               preferred_element_type=F32)

    a_vals = {k: first_matmul(k) for k in range(min(PEER_LOOKAHEAD, subs))}
    for sb in range(subs):
        if sb + PEER_LOOKAHEAD < subs:
            a_vals[sb + PEER_LOOKAHEAD] = first_matmul(sb + PEER_LOOKAHEAD)
        aT = a_vals.pop(sb)
        e1 = [(j * subs + sb) * rows + r for r in range(rows)]
        w_cols = [[] for _ in range(rows)]
        for c in range(chunks):
            w = [jnp.zeros((N_KEYS, LANES), BF16) for _ in range(rows)]
            for hd in range(PEER_HEADS):
                rk, pb = rank_ref[hd, c], pb_ref[hd, c]
                for r in range(rows):
                    cnt = jnp.broadcast_to(cnt_ref[hd, c, pl.ds(e1[r], 1), :], (BF16_ROWS, LANES)).astype(BF16)
                    pa = jnp.broadcast_to(pa_ref[hd, c, pl.ds(e1[r], 1), :], (BF16_ROWS, LANES)).astype(BF16)
                    cnt = jnp.concatenate([cnt] * tiles, axis=0)
                    pa = jnp.concatenate([pa] * tiles, axis=0)
                    w[r] = w[r] + jnp.where(rk < cnt, pb, zero) * pa
            for r in range(rows):
                w_cols[r].append(w[r])
        zs = []
        for r in range(rows):
            g = _gelu_times_two(aT[r * N_KEYS:(r + 1) * N_KEYS].astype(BF16))
            zs.append(g * jnp.concatenate(w_cols[r], axis=1))
        zT = jnp.concatenate(zs, axis=0)
        acc_ref[...] += jnp.dot(vT_ref[:, sb * PEER_SUB:(sb + 1) * PEER_SUB], zT,
                                preferred_element_type=F32)

    @pl.when(j == pl.num_programs(1) - 1)
    def _():
        o_ref[...] = x1_ref[...] + mod_ref[0, 5:6, :] * acc_ref[...].T


def _peer(h2, x1, mod, w_pq, sub_keys, peer_u, peer_v, seq):
    t, d = h2.shape
    tb, eb = PEER_TOKENS, PEER_EXPERTS
    n_exp = peer_u.shape[0]
    assert eb % PEER_SUB == 0 and PEER_SUB % N_KEYS == 0
    chunks = tb // LANES
    packed = pltpu.VMEM((PEER_HEADS, chunks, N_KEYS, LANES), BF16)
    rowtab = pltpu.VMEM((PEER_HEADS, chunks, N_KEYS, LANES), F32)
    scores = pltpu.VMEM((2, chunks, N_KEYS, LANES), F32)
    keys_hi = sub_keys.astype(BF16)
    keys_lo = (sub_keys.astype(F32) - keys_hi.astype(F32)).astype(BF16)
    return pl.pallas_call(
        _peer_kernel,
        grid=(t // tb, n_exp // eb),
        in_specs=[
            pl.BlockSpec((tb, d), lambda i, j: (i, 0)),
            pl.BlockSpec((tb, d), lambda i, j: (i, 0)),
            pl.BlockSpec((1, 6, d), lambda i, j: (i * tb // seq, 0, 0)),
            _full((w_pq.shape[1], d)),
            _full(sub_keys.shape),
            _full(sub_keys.shape),
            pl.BlockSpec((eb, d), lambda i, j: (j, 0)),
            pl.BlockSpec((d, eb), lambda i, j: (0, j)),
        ],
        out_specs=pl.BlockSpec((tb, d), lambda i, j: (i, 0)),
        out_shape=jax.ShapeDtypeStruct((t, d), F32),
        scratch_shapes=[packed, packed, rowtab, rowtab, scores, pltpu.VMEM((d, tb), F32)],
        compiler_params=_params("arbitrary", "arbitrary"),
        name="peer",
    )(h2, x1, mod, w_pq.T.astype(BF16), keys_hi, keys_lo, peer_u.astype(BF16), peer_v.T.astype(BF16))


def kernel(x, c, positions, w_ada, b_ada, norm1_g, w_in, mla_qa_g, mla_kva_g, w_uq, w_ukv, mla_q_g, mla_k_g,
           fox_q_g, fox_k_g, b_f, w_o, norm2_g, w_pq, sub_keys, peer_u, peer_v):
    b, s, d = x.shape
    assert s % PREP_TOKENS == 0 and s % ATTN_BLOCK == 0 and s % PROJ_TOKENS == 0 and s % PEER_TOKENS == 0
    assert PREP_TOKENS == ATTN_BLOCK and peer_u.shape[0] == N_KEYS * N_KEYS
    mod = _ada(c, w_ada, b_ada)
    qt, k, vt = _prep(x, positions, mod, norm1_g, w_in, mla_qa_g, mla_kva_g, w_uq, w_ukv, mla_q_g, mla_k_g,
                      fox_q_g, fox_k_g, b_f)
    ot = _attention(qt, k, vt)
    x1, h2 = _out_proj(ot, x, mod, w_o, norm2_g)
    out = _peer(h2.reshape(b * s, d), x1.reshape(b * s, d), mod, w_pq, sub_keys, peer_u, peer_v, s)
    return out.reshape(b, s, d).astype(x.dtype)
```

```python
import functools
import math

import jax
import jax.numpy as jnp
from jax import lax
from jax.experimental import pallas as pl
from jax.experimental.pallas import tpu as pltpu

F32 = jnp.float32
BF16 = jnp.bfloat16

EPS = 1e-6
LOG2E = 1.4426950408889634
ROPE_THETA = 10000.0

LANES = 128
SUBLANES = 8
VMEM_LIMIT_BYTES = 56 * 1024 * 1024

MLA_HEADS = 8
MLA_Q_RANK = 384
MLA_KV_RANK = 256
MLA_NOPE = 64
MLA_ROPE = 32
MLA_QK = MLA_NOPE + MLA_ROPE
MLA_V = 64
FOX_HEADS = 8
FOX_DIM = 64
FOX_WIDTH = FOX_HEADS * FOX_DIM
HEADS = MLA_HEADS + FOX_HEADS
HEAD_PAD = 128
AUG_ROWS = 16
V_ROWS = 80
PEER_HEADS = 8
N_KEYS = 128
PEER_HALF = 128
PEER_TOPK = 16
OFF_CQ = 0
OFF_CKV = OFF_CQ + MLA_Q_RANK
OFF_KPE = OFF_CKV + MLA_KV_RANK
OFF_FQ = OFF_KPE + MLA_ROPE
OFF_FK = OFF_FQ + FOX_WIDTH
OFF_FV = OFF_FK + FOX_WIDTH
OFF_FL = OFF_FV + FOX_WIDTH

PREP_TOKENS = 512
ATTN_BLOCK = 512
PROJ_TOKENS = 512
PEER_TOKENS = 512
PEER_EXPERTS = 2048
PEER_SUB = 512
PEER_LOOKAHEAD = 2
BF16_ROWS = 16

NT_DIMS = (((1,), (1,)), ((), ()))
TN_DIMS = (((0,), (0,)), ((), ()))


def _params(*semantics):
    return pltpu.CompilerParams(dimension_semantics=semantics, vmem_limit_bytes=VMEM_LIMIT_BYTES)


def _full(shape):
    n = len(shape)
    return pl.BlockSpec(shape, lambda *_: (0,) * n)


def _ada_kernel(c_ref, w_ref, b_ref, o_ref):
    c = c_ref[...]
    s = c / (1.0 + jnp.exp(-c))
    o_ref[...] = jnp.dot(s, w_ref[...], preferred_element_type=F32,
                         precision=lax.Precision.HIGHEST) + b_ref[...]


def _ada(c, w_ada, b_ada):
    b, d = c.shape
    n = w_ada.shape[1]
    rows = 8
    cp = jnp.zeros((rows, d), F32).at[:b].set(c.astype(F32))
    out = pl.pallas_call(
        _ada_kernel,
        grid=(n // d,),
        in_specs=[_full((rows, d)), pl.BlockSpec((d, d), lambda j: (0, j)), pl.BlockSpec((1, d), lambda j: (0, j))],
        out_specs=pl.BlockSpec((rows, d), lambda j: (0, j)),
        out_shape=jax.ShapeDtypeStruct((rows, n), F32),
        compiler_params=_params("arbitrary"),
        name="adaln",
    )(cp, w_ada.astype(F32), b_ada.astype(F32).reshape(1, n))
    return out[:b].reshape(b, n // d, d)


def _split3(f):
    hi = f.astype(BF16).astype(F32)
    r = f - hi
    mid = r.astype(BF16).astype(F32)
    return hi, mid, r - mid


def _aug_block(vals, ones_first, width):
    row = lax.broadcasted_iota(jnp.int32, (AUG_ROWS, width), 0)
    v0, v1, v2 = vals
    off = 3 if ones_first else 0
    blk = jnp.where(row == off, v0, jnp.where(row == off + 1, v1, jnp.where(row == off + 2, v2, 0.0)))
    one_lo = 0 if ones_first else 3
    return jnp.where((row >= one_lo) & (row < one_lo + 3), 1.0, blk)


def _prep_kernel(x_ref, pos_ref, mod_ref, g1_ref, winT_ref, gqa_ref, gkva_ref, wuqT_ref, wukvT_ref,
                 gq_ref, gk_ref, gfq_ref, gfk_ref, bf_ref, invf_ref,
                 qt_ref, k_ref, vt_ref, carry_ref):
    tm = x_ref.shape[1]

    @pl.when(pl.program_id(1) == 0)
    def _():
        carry_ref[...] = jnp.zeros_like(carry_ref)

    x = x_ref[0]
    y = x * lax.rsqrt(jnp.mean(x * x, axis=-1, keepdims=True) + EPS) * g1_ref[...]
    h = y * (1.0 + mod_ref[0, 1:2, :]) + mod_ref[0, 0:1, :]
    projT = lax.dot_general(winT_ref[...], h.astype(BF16), NT_DIMS, preferred_element_type=F32)

    def rms_rows(v, n):
        return lax.rsqrt(jnp.sum(v * v, axis=0, keepdims=True) * (1.0 / n) + EPS)

    cq = projT[OFF_CQ:OFF_CKV]
    cqn = (cq * rms_rows(cq, MLA_Q_RANK) * gqa_ref[...]).astype(BF16)
    qaT = jnp.dot(wuqT_ref[...], cqn, preferred_element_type=F32)
    ckv = projT[OFF_CKV:OFF_KPE]
    ckvn = (ckv * rms_rows(ckv, MLA_KV_RANK) * gkva_ref[...]).astype(BF16)
    kvT = jnp.dot(wukvT_ref[...], ckvn, preferred_element_type=F32)
    kpe = projT[OFF_KPE:OFF_FQ]

    ang = pos_ref[0].astype(F32) * invf_ref[...]
    cos, sin = jnp.cos(ang), jnp.sin(ang)
    half = MLA_ROPE // 2

    def rope(v):
        v1, v2 = v[:half], v[half:]
        return v1 * cos - v2 * sin, v2 * cos + v1 * sin

    gq, gk = gq_ref[...], gk_ref[...]
    kpe_ss = jnp.sum(kpe * kpe, axis=0, keepdims=True)
    kr1, kr2 = rope(kpe * gk[MLA_NOPE:])
    zpad_mla = jnp.zeros((HEAD_PAD - MLA_QK, tm), F32)
    ones_rows = jnp.ones((V_ROWS - MLA_V, tm), BF16)
    q_scale = MLA_QK ** -0.5 * LOG2E
    for hd in range(MLA_HEADS):
        q = qaT[hd * MLA_QK:(hd + 1) * MLA_QK]
        qn = q * rms_rows(q, MLA_QK) * gq
        r1, r2 = rope(qn[MLA_NOPE:])
        qt_ref[0, hd] = (jnp.concatenate([qn[:MLA_NOPE], r1, r2, zpad_mla], axis=0) * q_scale).astype(BF16)
        base = hd * (MLA_NOPE + MLA_V)
        kn = kvT[base:base + MLA_NOPE]
        r = lax.rsqrt((jnp.sum(kn * kn, axis=0, keepdims=True) + kpe_ss) * (1.0 / MLA_QK) + EPS)
        kblk = jnp.concatenate([kn * r * gk[:MLA_NOPE], kr1 * r, kr2 * r, zpad_mla], axis=0)
        k_ref[0, hd] = kblk.T.astype(BF16)
        vt_ref[0, hd, 0, :MLA_V] = kvT[base + MLA_NOPE:base + MLA_NOPE + MLA_V].astype(BF16)
        vt_ref[0, hd, 0, MLA_V:] = ones_rows

    z = projT[OFF_FL:OFF_FL + FOX_HEADS] + bf_ref[...]
    logf = -(jnp.maximum(-z, 0.0) + jnp.log1p(jnp.exp(-jnp.abs(z))))
    lhi, lmid, llo = _split3(logf)
    tri = (lax.broadcasted_iota(jnp.int32, (tm, tm), 0) <= lax.broadcasted_iota(jnp.int32, (tm, tm), 1))
    tri = jnp.where(tri, 1.0, 0.0).astype(BF16)
    parts = jnp.dot(jnp.concatenate([lhi, lmid, llo], axis=0).astype(BF16), tri, preferred_element_type=F32)
    fcum = (parts[2 * FOX_HEADS:] + parts[FOX_HEADS:2 * FOX_HEADS]) + parts[:FOX_HEADS] + carry_ref[:, 0:1]
    carry_ref[...] = jnp.broadcast_to(fcum[:, tm - 1:tm], carry_ref.shape)
    fc = fcum * LOG2E

    gfq, gfk = gfq_ref[...], gfk_ref[...]
    zpad_fox = jnp.zeros((HEAD_PAD - FOX_DIM - AUG_ROWS, tm), F32)
    f_scale = FOX_DIM ** -0.5 * LOG2E
    for hd in range(FOX_HEADS):
        fq = projT[OFF_FQ + hd * FOX_DIM:OFF_FQ + (hd + 1) * FOX_DIM]
        fk = projT[OFF_FK + hd * FOX_DIM:OFF_FK + (hd + 1) * FOX_DIM]
        fv = projT[OFF_FV + hd * FOX_DIM:OFF_FV + (hd + 1) * FOX_DIM]
        f_row = fc[hd:hd + 1]
        qn = fq * rms_rows(fq, FOX_DIM) * gfq * f_scale
        qblk = jnp.concatenate([qn, _aug_block(_split3(f_row), False, tm), zpad_fox], axis=0)
        qt_ref[0, MLA_HEADS + hd] = qblk.astype(BF16)
        kn = fk * rms_rows(fk, FOX_DIM) * gfk
        kblk = jnp.concatenate([kn, _aug_block(_split3(-f_row), True, tm), zpad_fox], axis=0)
        k_ref[0, MLA_HEADS + hd] = kblk.T.astype(BF16)
        vt_ref[0, MLA_HEADS + hd, 0, :MLA_V] = fv.astype(BF16)
        vt_ref[0, MLA_HEADS + hd, 0, MLA_V:] = ones_rows


def _prep(x, positions, mod, norm1_g, w_in, mla_qa_g, mla_kva_g, w_uq, w_ukv, mla_q_g, mla_k_g,
          fox_q_g, fox_k_g, b_f):
    b, s, d = x.shape
    tm = PREP_TOKENS
    nj = s // tm
    col = lambda v: v.astype(F32).reshape(-1, 1)
    inv_freq = ROPE_THETA ** (-jnp.arange(0, MLA_ROPE, 2, dtype=F32) / MLA_ROPE)
    in_width = w_in.shape[1]
    args = (x, positions.reshape(b, 1, s), mod, norm1_g.astype(F32).reshape(1, d), w_in.T.astype(BF16),
            col(mla_qa_g), col(mla_kva_g), w_uq.T.astype(BF16), w_ukv.T.astype(BF16),
            col(mla_q_g), col(mla_k_g), col(fox_q_g), col(fox_k_g), col(b_f), col(inv_freq))
    in_specs = [
        pl.BlockSpec((1, tm, d), lambda i, j: (i, j, 0)),
        pl.BlockSpec((1, 1, tm), lambda i, j: (i, 0, j)),
        pl.BlockSpec((1, 6, d), lambda i, j: (i, 0, 0)),
    ] + [_full(a.shape) for a in args[3:]]
    assert in_width == OFF_FL + FOX_HEADS
    return pl.pallas_call(
        _prep_kernel,
        grid=(b, nj),
        in_specs=in_specs,
        out_specs=[
            pl.BlockSpec((1, HEADS, HEAD_PAD, tm), lambda i, j: (i, 0, 0, j)),
            pl.BlockSpec((1, HEADS, tm, HEAD_PAD), lambda i, j: (i, 0, j, 0)),
            pl.BlockSpec((1, HEADS, 1, V_ROWS, tm), lambda i, j: (i, 0, j, 0, 0)),
        ],
        out_shape=[
            jax.ShapeDtypeStruct((b, HEADS, HEAD_PAD, s), BF16),
            jax.ShapeDtypeStruct((b, HEADS, s, HEAD_PAD), BF16),
            jax.ShapeDtypeStruct((b, HEADS, nj, V_ROWS, tm), BF16),
        ],
        scratch_shapes=[pltpu.VMEM((FOX_HEADS, LANES), F32)],
        compiler_params=_params("arbitrary", "arbitrary"),
        name="token_prep",
    )(*args)


def _attn_kernel(qt_ref, k_ref, vt_ref, o_ref, s_ref):
    blk = qt_ref.shape[3]
    i = pl.program_id(2)
    qt = qt_ref[0, 0]

    def scores(slot, j):
        kb = k_ref[0, 0, pl.ds(pl.multiple_of(j * blk, blk), blk), :]
        s_ref[slot] = jnp.dot(kb, qt, preferred_element_type=F32)

    def consume(slot, j, carry, diagonal):
        m, acc = carry
        s = s_ref[slot]
        if diagonal:
            keep = lax.broadcasted_iota(jnp.int32, (blk, blk), 0) <= lax.broadcasted_iota(jnp.int32, (blk, blk), 1)
            s = jnp.where(keep, s, -jnp.inf)
        m_new = jnp.maximum(m, jnp.max(s, axis=0, keepdims=True))
        p = jnp.exp2(s - m_new).astype(BF16)
        acc = jnp.exp2(m - m_new) * acc + jnp.dot(vt_ref[0, 0, j], p, preferred_element_type=F32)
        return m_new, acc

    def finish(carry):
        _, acc = carry
        o_ref[0] = (acc[:MLA_V] / acc[MLA_V:MLA_V + 1]).astype(o_ref.dtype)

    def pair(jj, carry):
        j0 = 2 * jj
        scores(1, j0 + 1)
        carry = consume(0, j0, carry, False)
        scores(0, j0 + 2)
        return consume(1, j0 + 1, carry, False)

    init = (jnp.full((1, blk), -jnp.inf, F32), jnp.zeros((vt_ref.shape[3], blk), F32))
    scores(0, 0)
    carry = lax.fori_loop(0, i // 2, pair, init)

    @pl.when(i % 2 == 0)
    def _():
        finish(consume(0, i, carry, True))

    @pl.when(i % 2 == 1)
    def _():
        scores(1, i)
        finish(consume(1, i, consume(0, i - 1, carry, False), True))


def _attention(qt, k, vt):
    b, heads, _, s = qt.shape
    blk = ATTN_BLOCK
    assert vt.shape[4] == blk
    return pl.pallas_call(
        _attn_kernel,
        grid=(b, heads, s // blk),
        in_specs=[
            pl.BlockSpec((1, 1, HEAD_PAD, blk), lambda bi, h, i: (bi, h, 0, i)),
            pl.BlockSpec((1, 1, s, HEAD_PAD), lambda bi, h, i: (bi, h, 0, 0)),
            pl.BlockSpec((1, 1, s // blk, V_ROWS, blk), lambda bi, h, i: (bi, h, 0, 0, 0)),
        ],
        out_specs=pl.BlockSpec((1, MLA_V, blk), lambda bi, h, i: (bi, h, i)),
        out_shape=jax.ShapeDtypeStruct((b, heads * MLA_V, s), BF16),
        scratch_shapes=[pltpu.VMEM((2, blk, blk), F32)],
        compiler_params=_params("arbitrary", "arbitrary", "arbitrary"),
        name="flash_attention",
    )(qt, k, vt)


def _proj_kernel(ot_ref, x_ref, mod_ref, wo_ref, g2_ref, x1_ref, h2_ref):
    attn = lax.dot_general(ot_ref[0], wo_ref[...], TN_DIMS, preferred_element_type=F32)
    x1 = x_ref[0] + mod_ref[0, 2:3, :] * attn
    x1_ref[0] = x1
    y = x1 * lax.rsqrt(jnp.mean(x1 * x1, axis=-1, keepdims=True) + EPS) * g2_ref[...]
    h2_ref[0] = (y * (1.0 + mod_ref[0, 4:5, :]) + mod_ref[0, 3:4, :]).astype(h2_ref.dtype)


def _out_proj(ot, x, mod, w_o, norm2_g):
    b, s, d = x.shape
    tm = PROJ_TOKENS
    width = ot.shape[1]
    return pl.pallas_call(
        _proj_kernel,
        grid=(b, s // tm),
        in_specs=[
            pl.BlockSpec((1, width, tm), lambda i, j: (i, 0, j)),
            pl.BlockSpec((1, tm, d), lambda i, j: (i, j, 0)),
            pl.BlockSpec((1, 6, d), lambda i, j: (i, 0, 0)),
            _full((width, d)),
            _full((1, d)),
        ],
        out_specs=[pl.BlockSpec((1, tm, d), lambda i, j: (i, j, 0))] * 2,
        out_shape=[jax.ShapeDtypeStruct((b, s, d), F32), jax.ShapeDtypeStruct((b, s, d), BF16)],
        compiler_params=_params("arbitrary", "arbitrary"),
        name="out_proj",
    )(ot, x, mod, w_o.astype(BF16), norm2_g.astype(F32).reshape(1, d))


def _batcher_pairs(n):
    pairs = []
    p = 1
    while p < n:
        k = p
        while k >= 1:
            for j in range(k % p, n - k, 2 * k):
                for i in range(min(k, n - j - k)):
                    if (i + j) // (2 * p) == (i + j + k) // (2 * p):
                        pairs.append((i + j, i + j + k))
            k //= 2
        p *= 2
    return pairs


SORT16 = _batcher_pairs(PEER_TOPK)
BITONIC16 = [(i, i + d) for d in (8, 4, 2, 1) for i in range(PEER_TOPK) if (i // d) % 2 == 0]


def _exchange(x, pairs):
    for i, j in pairs:
        x[i], x[j] = jnp.maximum(x[i], x[j]), jnp.minimum(x[i], x[j])
    return x


def _top16_sorted(x):
    x = _exchange(list(x), SORT16)
    for shift in (4, 2, 1):
        y = [jnp.maximum(x[k], pltpu.roll(x[PEER_TOPK - 1 - k], shift, 0)) for k in range(PEER_TOPK)]
        x = _exchange(y, BITONIC16)
    return x


def _search16(test, vb):
    c1 = test(vb[7])
    c2 = test(jnp.where(c1, vb[11], vb[3]))
    c3 = test(jnp.where(c1, jnp.where(c2, vb[13], vb[9]), jnp.where(c2, vb[5], vb[1])))
    c4 = test(jnp.where(c1,
                        jnp.where(c2, jnp.where(c3, vb[14], vb[12]), jnp.where(c3, vb[10], vb[8])),
                        jnp.where(c2, jnp.where(c3, vb[6], vb[4]), jnp.where(c3, vb[2], vb[0]))))
    return (jnp.where(c1, 8.0, 0.0) + jnp.where(c2, 4.0, 0.0)) + (jnp.where(c3, 2.0, 0.0) + jnp.where(c4, 1.0, 0.0))


def _route(s1, s2):
    n = s1.shape[0] // SUBLANES
    x1 = [s1[SUBLANES * r:SUBLANES * (r + 1)] for r in range(n)]
    x2 = [s2[SUBLANES * r:SUBLANES * (r + 1)] for r in range(n)]
    v1 = _top16_sorted(x1)
    v2 = _top16_sorted(x2)
    sub = lax.broadcasted_iota(jnp.int32, v1[0].shape, 0)

    def spread(v, base):
        out = v[base + SUBLANES - 1]
        for k in range(SUBLANES - 2, -1, -1):
            out = jnp.where(sub == k, v[base + k], out)
        return out

    v2lo, v2hi, v1hi = spread(v2, 0), spread(v2, SUBLANES), spread(v1, SUBLANES)
    cand = [v1[0] + v2lo, v1[0] + v2hi] + [v1[i] + v2lo for i in range(1, SUBLANES)] + [v1hi + v2[0]]
    cand += [jnp.full_like(v1[0], -jnp.inf)] * (PEER_TOPK - len(cand))
    best = _top16_sorted(cand)
    tau, top = best[PEER_TOPK - 1], best[0]
    zsum = jnp.exp(best[0] - top)
    for k in range(1, PEER_TOPK):
        zsum = zsum + jnp.exp(best[k] - top)
    half_inv = 0.5 / zsum
    rank2, count, pa_half, pb = [], [], [], []
    for r in range(n):
        a, b = x1[r], x2[r]
        rk = _search16(lambda t: b < t, v2)
        rank2.append(jnp.where(b < v2[PEER_TOPK - 1], float(PEER_TOPK), rk))
        ct = _search16(lambda t: a + t >= tau, v2)
        count.append(jnp.where(a + v2[PEER_TOPK - 1] >= tau, float(PEER_TOPK), ct))
        pa_half.append(jnp.exp(a - v1[0]) * half_inv)
        pb.append(jnp.exp(b - v2[0]))
    cat = lambda xs: jnp.concatenate(xs, axis=0)
    return cat(rank2), cat(count), cat(pa_half), cat(pb)


GELU_C1 = math.sqrt(2.0 / math.pi)
GELU_C2 = GELU_C1 * 0.044715


def _gelu_times_two(a):
    t = jnp.tanh(a * (jnp.asarray(GELU_C1, a.dtype) + jnp.asarray(GELU_C2, a.dtype) * (a * a)))
    return a + a * t


def _peer_kernel(h2_ref, x1_ref, mod_ref, wpqT_ref, khi_ref, klo_ref, u_ref, vT_ref, o_ref,
                 rank_ref, pb_ref, cnt_ref, pa_ref, sc_ref, acc_ref):
    j = pl.program_id(1)
    tb = h2_ref.shape[0]
    subs = u_ref.shape[0] // PEER_SUB
    chunks = tb // LANES
    tiles = N_KEYS // BF16_ROWS
    h2 = h2_ref[...]

    @pl.when(j == 0)
    def _():
        acc_ref[...] = jnp.zeros_like(acc_ref)

        def head_body(hd, carry):
            w_rows = wpqT_ref[pl.ds(pl.multiple_of(hd * 2 * PEER_HALF, 2 * PEER_HALF), 2 * PEER_HALF), :]
            qpT = lax.dot_general(w_rows, h2, NT_DIMS, preferred_element_type=F32)
            q_hi = qpT.astype(BF16)
            q_lo = (qpT - q_hi.astype(F32)).astype(BF16)
            for half in range(2):
                rows_q = slice(half * PEER_HALF, (half + 1) * PEER_HALF)
                k_hi, k_lo = khi_ref[hd, half], klo_ref[hd, half]
                sc = jnp.dot(k_hi, q_hi[rows_q], preferred_element_type=F32) + (
                    jnp.dot(k_hi, q_lo[rows_q], preferred_element_type=F32)
                    + jnp.dot(k_lo, q_hi[rows_q], preferred_element_type=F32))
                for c in range(chunks):
                    sc_ref[half, c] = sc[:, c * LANES:(c + 1) * LANES]

            def chunk_body(c, carry2):
                rank2, count, pa_half, pb = _route(sc_ref[0, c], sc_ref[1, c])
                rank_ref[hd, c] = rank2.astype(BF16)
                pb_ref[hd, c] = pb.astype(BF16)
                cnt_ref[hd, c] = count
                pa_ref[hd, c] = pa_half
                return carry2

            return lax.fori_loop(0, chunks, chunk_body, carry)

        lax.fori_loop(0, PEER_HEADS, head_body, 0)

    rows = PEER_SUB // N_KEYS
    zero = jnp.zeros((), BF16)

    def first_matmul(sb):
        return lax.dot_general(u_ref[sb * PEER_SUB:(sb + 1) * PEER_SUB, :], h2, NT_DIMS,
                               preferred_element_type=F32)

    a_vals = {k: first_matmul(k) for k in range(min(PEER_LOOKAHEAD, subs))}
    for sb in range(subs):
        if sb + PEER_LOOKAHEAD < subs:
            a_vals[sb + PEER_LOOKAHEAD] = first_matmul(sb + PEER_LOOKAHEAD)
        aT = a_vals.pop(sb)
        e1 = [(j * subs + sb) * rows + r for r in range(rows)]
        w_cols = [[] for _ in range(rows)]
        for c in range(chunks):
            w = [jnp.zeros((N_KEYS, LANES), BF16) for _ in range(rows)]
            for hd in range(PEER_HEADS):
                rk, pb = rank_ref[hd, c], pb_ref[hd, c]
                for r in range(rows):
                    cnt = jnp.broadcast_to(cnt_ref[hd, c, pl.ds(e1[r], 1), :], (BF16_ROWS, LANES)).astype(BF16)
                    pa = jnp.broadcast_to(pa_ref[hd, c, pl.ds(e1[r], 1), :], (BF16_ROWS, LANES)).astype(BF16)
                    cnt = jnp.concatenate([cnt] * tiles, axis=0)
                    pa = jnp.concatenate([pa] * tiles, axis=0)
                    w[r] = w[r] + jnp.where(rk < cnt, pb, zero) * pa
            for r in range(rows):
                w_cols[r].append(w[r])
        zs = []
        for r in range(rows):
            g = _gelu_times_two(aT[r * N_KEYS:(r + 1) * N_KEYS].astype(BF16))
            zs.append(g * jnp.concatenate(w_cols[r], axis=1))
        zT = jnp.concatenate(zs, axis=0)
        acc_ref[...] += jnp.dot(vT_ref[:, sb * PEER_SUB:(sb + 1) * PEER_SUB], zT,
                                preferred_element_type=F32)

    @pl.when(j == pl.num_programs(1) - 1)
    def _():
        o_ref[...] = x1_ref[...] + mod_ref[0, 5:6, :] * acc_ref[...].T


def _peer(h2, x1, mod, w_pq, sub_keys, peer_u, peer_v, seq):
    t, d = h2.shape
    tb, eb = PEER_TOKENS, PEER_EXPERTS
    n_exp = peer_u.shape[0]
    assert eb % PEER_SUB == 0 and PEER_SUB % N_KEYS == 0
    chunks = tb // LANES
    packed = pltpu.VMEM((PEER_HEADS, chunks, N_KEYS, LANES), BF16)
    rowtab = pltpu.VMEM((PEER_HEADS, chunks, N_KEYS, LANES), F32)
    scores = pltpu.VMEM((2, chunks, N_KEYS, LANES), F32)
    keys_hi = sub_keys.astype(BF16)
    keys_lo = (sub_keys.astype(F32) - keys_hi.astype(F32)).astype(BF16)
    return pl.pallas_call(
        _peer_kernel,
        grid=(t // tb, n_exp // eb),
        in_specs=[
            pl.BlockSpec((tb, d), lambda i, j: (i, 0)),
            pl.BlockSpec((tb, d), lambda i, j: (i, 0)),
            pl.BlockSpec((1, 6, d), lambda i, j: (i * tb // seq, 0, 0)),
            _full((w_pq.shape[1], d)),
            _full(sub_keys.shape),
            _full(sub_keys.shape),
            pl.BlockSpec((eb, d), lambda i, j: (j, 0)),
            pl.BlockSpec((d, eb), lambda i, j: (0, j)),
        ],
        out_specs=pl.BlockSpec((tb, d), lambda i, j: (i, 0)),
        out_shape=jax.ShapeDtypeStruct((t, d), F32),
        scratch_shapes=[packed, packed, rowtab, rowtab, scores, pltpu.VMEM((d, tb), F32)],
        compiler_params=_params("arbitrary", "arbitrary"),
        name="peer",
    )(h2, x1, mod, w_pq.T.astype(BF16), keys_hi, keys_lo, peer_u.astype(BF16), peer_v.T.astype(BF16))


def kernel(x, c, positions, w_ada, b_ada, norm1_g, w_in, mla_qa_g, mla_kva_g, w_uq, w_ukv, mla_q_g, mla_k_g,
           fox_q_g, fox_k_g, b_f, w_o, norm2_g, w_pq, sub_keys, peer_u, peer_v):
    b, s, d = x.shape
    assert s % PREP_TOKENS == 0 and s % ATTN_BLOCK == 0 and s % PROJ_TOKENS == 0 and s % PEER_TOKENS == 0
    assert PREP_TOKENS == ATTN_BLOCK and peer_u.shape[0] == N_KEYS * N_KEYS
    mod = _ada(c, w_ada, b_ada)
    qt, k, vt = _prep(x, positions, mod, norm1_g, w_in, mla_qa_g, mla_kva_g, w_uq, w_ukv, mla_q_g, mla_k_g,
                      fox_q_g, fox_k_g, b_f)
    ot = _attention(qt, k, vt)
    x1, h2 = _out_proj(ot, x, mod, w_o, norm2_g)
    out = _peer(h2.reshape(b * s, d), x1.reshape(b * s, d), mod, w_pq, sub_keys, peer_u, peer_v, s)
    return out.reshape(b, s, d).astype(x.dtype)
```

```python
import functools
import math

import jax
import jax.numpy as jnp
from jax import lax
from jax.experimental import pallas as pl
from jax.experimental.pallas import tpu as pltpu

F32 = jnp.float32
BF16 = jnp.bfloat16

EPS = 1e-6
LOG2E = 1.4426950408889634
ROPE_THETA = 10000.0

LANES = 128
SUBLANES = 8
VMEM_LIMIT_BYTES = 56 * 1024 * 1024

MLA_HEADS = 8
MLA_Q_RANK = 384
MLA_KV_RANK = 256
MLA_NOPE = 64
MLA_ROPE = 32
MLA_QK = MLA_NOPE + MLA_ROPE
MLA_V = 64
FOX_HEADS = 8
FOX_DIM = 64
FOX_WIDTH = FOX_HEADS * FOX_DIM
HEADS = MLA_HEADS + FOX_HEADS
HEAD_PAD = 128
AUG_ROWS = 16
V_ROWS = 80
PEER_HEADS = 8
N_KEYS = 128
PEER_HALF = 128
PEER_TOPK = 16
OFF_CQ = 0
OFF_CKV = OFF_CQ + MLA_Q_RANK
OFF_KPE = OFF_CKV + MLA_KV_RANK
OFF_FQ = OFF_KPE + MLA_ROPE
OFF_FK = OFF_FQ + FOX_WIDTH
OFF_FV = OFF_FK + FOX_WIDTH
OFF_FL = OFF_FV + FOX_WIDTH

PREP_TOKENS = 512
ATTN_BLOCK = 512
PROJ_TOKENS = 512
PEER_TOKENS = 512
PEER_EXPERTS = 2048
PEER_SUB = 1024
PEER_LOOKAHEAD = 1
BF16_ROWS = 16

NT_DIMS = (((1,), (1,)), ((), ()))
TN_DIMS = (((0,), (0,)), ((), ()))


def _params(*semantics):
    return pltpu.CompilerParams(dimension_semantics=semantics, vmem_limit_bytes=VMEM_LIMIT_BYTES)


def _full(shape):
    n = len(shape)
    return pl.BlockSpec(shape, lambda *_: (0,) * n)


def _ada_kernel(c_ref, w_ref, b_ref, o_ref):
    c = c_ref[...]
    s = c / (1.0 + jnp.exp(-c))
    o_ref[...] = jnp.dot(s, w_ref[...], preferred_element_type=F32,
                         precision=lax.Precision.HIGHEST) + b_ref[...]


def _ada(c, w_ada, b_ada):
    b, d = c.shape
    n = w_ada.shape[1]
    rows = 8
    cp = jnp.zeros((rows, d), F32).at[:b].set(c.astype(F32))
    out = pl.pallas_call(
        _ada_kernel,
        grid=(n // d,),
        in_specs=[_full((rows, d)), pl.BlockSpec((d, d), lambda j: (0, j)), pl.BlockSpec((1, d), lambda j: (0, j))],
        out_specs=pl.BlockSpec((rows, d), lambda j: (0, j)),
        out_shape=jax.ShapeDtypeStruct((rows, n), F32),
        compiler_params=_params("arbitrary"),
        name="adaln",
    )(cp, w_ada.astype(F32), b_ada.astype(F32).reshape(1, n))
    return out[:b].reshape(b, n // d, d)


def _split3(f):
    hi = f.astype(BF16).astype(F32)
    r = f - hi
    mid = r.astype(BF16).astype(F32)
    return hi, mid, r - mid


def _aug_block(vals, ones_first, width):
    row = lax.broadcasted_iota(jnp.int32, (AUG_ROWS, width), 0)
    v0, v1, v2 = vals
    off = 3 if ones_first else 0
    blk = jnp.where(row == off, v0, jnp.where(row == off + 1, v1, jnp.where(row == off + 2, v2, 0.0)))
    one_lo = 0 if ones_first else 3
    return jnp.where((row >= one_lo) & (row < one_lo + 3), 1.0, blk)


def _prep_kernel(x_ref, pos_ref, mod_ref, g1_ref, winT_ref, gqa_ref, gkva_ref, wuqT_ref, wukvT_ref,
                 gq_ref, gk_ref, gfq_ref, gfk_ref, bf_ref, invf_ref,
                 qt_ref, k_ref, vt_ref, carry_ref):
    tm = x_ref.shape[1]

    @pl.when(pl.program_id(1) == 0)
    def _():
        carry_ref[...] = jnp.zeros_like(carry_ref)

    x = x_ref[0]
    y = x * lax.rsqrt(jnp.mean(x * x, axis=-1, keepdims=True) + EPS) * g1_ref[...]
    h = y * (1.0 + mod_ref[0, 1:2, :]) + mod_ref[0, 0:1, :]
    projT = lax.dot_general(winT_ref[...], h.astype(BF16), NT_DIMS, preferred_element_type=F32)

    def rms_rows(v, n):
        return lax.rsqrt(jnp.sum(v * v, axis=0, keepdims=True) * (1.0 / n) + EPS)

    cq = projT[OFF_CQ:OFF_CKV]
    cqn = (cq * rms_rows(cq, MLA_Q_RANK) * gqa_ref[...]).astype(BF16)
    qaT = jnp.dot(wuqT_ref[...], cqn, preferred_element_type=F32)
    ckv = projT[OFF_CKV:OFF_KPE]
    ckvn = (ckv * rms_rows(ckv, MLA_KV_RANK) * gkva_ref[...]).astype(BF16)
    kvT = jnp.dot(wukvT_ref[...], ckvn, preferred_element_type=F32)
    kpe = projT[OFF_KPE:OFF_FQ]

    ang = pos_ref[0].astype(F32) * invf_ref[...]
    cos, sin = jnp.cos(ang), jnp.sin(ang)
    half = MLA_ROPE // 2

    def rope(v):
        v1, v2 = v[:half], v[half:]
        return v1 * cos - v2 * sin, v2 * cos + v1 * sin

    gq, gk = gq_ref[...], gk_ref[...]
    kpe_ss = jnp.sum(kpe * kpe, axis=0, keepdims=True)
    kr1, kr2 = rope(kpe * gk[MLA_NOPE:])
    zpad_mla = jnp.zeros((HEAD_PAD - MLA_QK, tm), F32)
    ones_rows = jnp.ones((V_ROWS - MLA_V, tm), BF16)
    q_scale = MLA_QK ** -0.5 * LOG2E
    for hd in range(MLA_HEADS):
        q = qaT[hd * MLA_QK:(hd + 1) * MLA_QK]
        qn = q * rms_rows(q, MLA_QK) * gq
        r1, r2 = rope(qn[MLA_NOPE:])
        qt_ref[0, hd] = (jnp.concatenate([qn[:MLA_NOPE], r1, r2, zpad_mla], axis=0) * q_scale).astype(BF16)
        base = hd * (MLA_NOPE + MLA_V)
        kn = kvT[base:base + MLA_NOPE]
        r = lax.rsqrt((jnp.sum(kn * kn, axis=0, keepdims=True) + kpe_ss) * (1.0 / MLA_QK) + EPS)
        kblk = jnp.concatenate([kn * r * gk[:MLA_NOPE], kr1 * r, kr2 * r, zpad_mla], axis=0)
        k_ref[0, hd] = kblk.T.astype(BF16)
        vt_ref[0, hd, 0, :MLA_V] = kvT[base + MLA_NOPE:base + MLA_NOPE + MLA_V].astype(BF16)
        vt_ref[0, hd, 0, MLA_V:] = ones_rows

    z = projT[OFF_FL:OFF_FL + FOX_HEADS] + bf_ref[...]
    logf = -(jnp.maximum(-z, 0.0) + jnp.log1p(jnp.exp(-jnp.abs(z))))
    lhi, lmid, llo = _split3(logf)
    tri = (lax.broadcasted_iota(jnp.int32, (tm, tm), 0) <= lax.broadcasted_iota(jnp.int32, (tm, tm), 1))
    tri = jnp.where(tri, 1.0, 0.0).astype(BF16)
    parts = jnp.dot(jnp.concatenate([lhi, lmid, llo], axis=0).astype(BF16), tri, preferred_element_type=F32)
    fcum = (parts[2 * FOX_HEADS:] + parts[FOX_HEADS:2 * FOX_HEADS]) + parts[:FOX_HEADS] + carry_ref[:, 0:1]
    carry_ref[...] = jnp.broadcast_to(fcum[:, tm - 1:tm], carry_ref.shape)
    fc = fcum * LOG2E

    gfq, gfk = gfq_ref[...], gfk_ref[...]
    zpad_fox = jnp.zeros((HEAD_PAD - FOX_DIM - AUG_ROWS, tm), F32)
    f_scale = FOX_DIM ** -0.5 * LOG2E
    for hd in range(FOX_HEADS):
        fq = projT[OFF_FQ + hd * FOX_DIM:OFF_FQ + (hd + 1) * FOX_DIM]
        fk = projT[OFF_FK + hd * FOX_DIM:OFF_FK + (hd + 1) * FOX_DIM]
        fv = projT[OFF_FV + hd * FOX_DIM:OFF_FV + (hd + 1) * FOX_DIM]
        f_row = fc[hd:hd + 1]
        qn = fq * rms_rows(fq, FOX_DIM) * gfq * f_scale
        qblk = jnp.concatenate([qn, _aug_block(_split3(f_row), False, tm), zpad_fox], axis=0)
        qt_ref[0, MLA_HEADS + hd] = qblk.astype(BF16)
        kn = fk * rms_rows(fk, FOX_DIM) * gfk
        kblk = jnp.concatenate([kn, _aug_block(_split3(-f_row), True, tm), zpad_fox], axis=0)
        k_ref[0, MLA_HEADS + hd] = kblk.T.astype(BF16)
        vt_ref[0, MLA_HEADS + hd, 0, :MLA_V] = fv.astype(BF16)
        vt_ref[0, MLA_HEADS + hd, 0, MLA_V:] = ones_rows


def _prep(x, positions, mod, norm1_g, w_in, mla_qa_g, mla_kva_g, w_uq, w_ukv, mla_q_g, mla_k_g,
          fox_q_g, fox_k_g, b_f):
    b, s, d = x.shape
    tm = PREP_TOKENS
    nj = s // tm
    col = lambda v: v.astype(F32).reshape(-1, 1)
    inv_freq = ROPE_THETA ** (-jnp.arange(0, MLA_ROPE, 2, dtype=F32) / MLA_ROPE)
    in_width = w_in.shape[1]
    args = (x, positions.reshape(b, 1, s), mod, norm1_g.astype(F32).reshape(1, d), w_in.T.astype(BF16),
            col(mla_qa_g), col(mla_kva_g), w_uq.T.astype(BF16), w_ukv.T.astype(BF16),
            col(mla_q_g), col(mla_k_g), col(fox_q_g), col(fox_k_g), col(b_f), col(inv_freq))
    in_specs = [
        pl.BlockSpec((1, tm, d), lambda i, j: (i, j, 0)),
        pl.BlockSpec((1, 1, tm), lambda i, j: (i, 0, j)),
        pl.BlockSpec((1, 6, d), lambda i, j: (i, 0, 0)),
    ] + [_full(a.shape) for a in args[3:]]
    assert in_width == OFF_FL + FOX_HEADS
    return pl.pallas_call(
        _prep_kernel,
        grid=(b, nj),
        in_specs=in_specs,
        out_specs=[
            pl.BlockSpec((1, HEADS, HEAD_PAD, tm), lambda i, j: (i, 0, 0, j)),
            pl.BlockSpec((1, HEADS, tm, HEAD_PAD), lambda i, j: (i, 0, j, 0)),
            pl.BlockSpec((1, HEADS, 1, V_ROWS, tm), lambda i, j: (i, 0, j, 0, 0)),
        ],
        out_shape=[
            jax.ShapeDtypeStruct((b, HEADS, HEAD_PAD, s), BF16),
            jax.ShapeDtypeStruct((b, HEADS, s, HEAD_PAD), BF16),
            jax.ShapeDtypeStruct((b, HEADS, nj, V_ROWS, tm), BF16),
        ],
        scratch_shapes=[pltpu.VMEM((FOX_HEADS, LANES), F32)],
        compiler_params=_params("arbitrary", "arbitrary"),
        name="token_prep",
    )(*args)


def _attn_kernel(qt_ref, k_ref, vt_ref, o_ref, s_ref):
    blk = qt_ref.shape[3]
    i = pl.program_id(2)
    qt = qt_ref[0, 0]

    def scores(slot, j):
        kb = k_ref[0, 0, pl.ds(pl.multiple_of(j * blk, blk), blk), :]
        s_ref[slot] = jnp.dot(kb, qt, preferred_element_type=F32)

    def consume(slot, j, carry, diagonal):
        m, acc = carry
        s = s_ref[slot]
        if diagonal:
            keep = lax.broadcasted_iota(jnp.int32, (blk, blk), 0) <= lax.broadcasted_iota(jnp.int32, (blk, blk), 1)
            s = jnp.where(keep, s, -jnp.inf)
        m_new = jnp.maximum(m, jnp.max(s, axis=0, keepdims=True))
        p = jnp.exp2(s - m_new).astype(BF16)
        acc = jnp.exp2(m - m_new) * acc + jnp.dot(vt_ref[0, 0, j], p, preferred_element_type=F32)
        return m_new, acc

    def finish(carry):
        _, acc = carry
        o_ref[0] = (acc[:MLA_V] / acc[MLA_V:MLA_V + 1]).astype(o_ref.dtype)

    def pair(jj, carry):
        j0 = 2 * jj
        scores(1, j0 + 1)
        carry = consume(0, j0, carry, False)
        scores(0, j0 + 2)
        return consume(1, j0 + 1, carry, False)

    init = (jnp.full((1, blk), -jnp.inf, F32), jnp.zeros((vt_ref.shape[3], blk), F32))
    scores(0, 0)
    carry = lax.fori_loop(0, i // 2, pair, init)

    @pl.when(i % 2 == 0)
    def _():
        finish(consume(0, i, carry, True))

    @pl.when(i % 2 == 1)
    def _():
        scores(1, i)
        finish(consume(1, i, consume(0, i - 1, carry, False), True))


def _attention(qt, k, vt):
    b, heads, _, s = qt.shape
    blk = ATTN_BLOCK
    assert vt.shape[4] == blk
    return pl.pallas_call(
        _attn_kernel,
        grid=(b, heads, s // blk),
        in_specs=[
            pl.BlockSpec((1, 1, HEAD_PAD, blk), lambda bi, h, i: (bi, h, 0, i)),
            pl.BlockSpec((1, 1, s, HEAD_PAD), lambda bi, h, i: (bi, h, 0, 0)),
            pl.BlockSpec((1, 1, s // blk, V_ROWS, blk), lambda bi, h, i: (bi, h, 0, 0, 0)),
        ],
        out_specs=pl.BlockSpec((1, MLA_V, blk), lambda bi, h, i: (bi, h, i)),
        out_shape=jax.ShapeDtypeStruct((b, heads * MLA_V, s), BF16),
        scratch_shapes=[pltpu.VMEM((2, blk, blk), F32)],
        compiler_params=_params("arbitrary", "arbitrary", "arbitrary"),
        name="flash_attention",
    )(qt, k, vt)


def _proj_kernel(ot_ref, x_ref, mod_ref, wo_ref, g2_ref, x1_ref, h2_ref):
    attn = lax.dot_general(ot_ref[0], wo_ref[...], TN_DIMS, preferred_element_type=F32)
    x1 = x_ref[0] + mod_ref[0, 2:3, :] * attn
    x1_ref[0] = x1
    y = x1 * lax.rsqrt(jnp.mean(x1 * x1, axis=-1, keepdims=True) + EPS) * g2_ref[...]
    h2_ref[0] = (y * (1.0 + mod_ref[0, 4:5, :]) + mod_ref[0, 3:4, :]).astype(h2_ref.dtype)


def _out_proj(ot, x, mod, w_o, norm2_g):
    b, s, d = x.shape
    tm = PROJ_TOKENS
    width = ot.shape[1]
    return pl.pallas_call(
        _proj_kernel,
        grid=(b, s // tm),
        in_specs=[
            pl.BlockSpec((1, width, tm), lambda i, j: (i, 0, j)),
            pl.BlockSpec((1, tm, d), lambda i, j: (i, j, 0)),
            pl.BlockSpec((1, 6, d), lambda i, j: (i, 0, 0)),
            _full((width, d)),
            _full((1, d)),
        ],
        out_specs=[pl.BlockSpec((1, tm, d), lambda i, j: (i, j, 0))] * 2,
        out_shape=[jax.ShapeDtypeStruct((b, s, d), F32), jax.ShapeDtypeStruct((b, s, d), BF16)],
        compiler_params=_params("arbitrary", "arbitrary"),
        name="out_proj",
    )(ot, x, mod, w_o.astype(BF16), norm2_g.astype(F32).reshape(1, d))


def _batcher_pairs(n):
    pairs = []
    p = 1
    while p < n:
        k = p
        while k >= 1:
            for j in range(k % p, n - k, 2 * k):
                for i in range(min(k, n - j - k)):
                    if (i + j) // (2 * p) == (i + j + k) // (2 * p):
                        pairs.append((i + j, i + j + k))
            k //= 2
        p *= 2
    return pairs


SORT16 = _batcher_pairs(PEER_TOPK)
BITONIC16 = [(i, i + d) for d in (8, 4, 2, 1) for i in range(PEER_TOPK) if (i // d) % 2 == 0]


def _exchange(x, pairs):
    for i, j in pairs:
        x[i], x[j] = jnp.maximum(x[i], x[j]), jnp.minimum(x[i], x[j])
    return x


def _top16_sorted(x):
    x = _exchange(list(x), SORT16)
    for shift in (4, 2, 1):
        y = [jnp.maximum(x[k], pltpu.roll(x[PEER_TOPK - 1 - k], shift, 0)) for k in range(PEER_TOPK)]
        x = _exchange(y, BITONIC16)
    return x


def _search16(test, vb):
    c1 = test(vb[7])
    c2 = test(jnp.where(c1, vb[11], vb[3]))
    c3 = test(jnp.where(c1, jnp.where(c2, vb[13], vb[9]), jnp.where(c2, vb[5], vb[1])))
    c4 = test(jnp.where(c1,
                        jnp.where(c2, jnp.where(c3, vb[14], vb[12]), jnp.where(c3, vb[10], vb[8])),
                        jnp.where(c2, jnp.where(c3, vb[6], vb[4]), jnp.where(c3, vb[2], vb[0]))))
    return (jnp.where(c1, 8.0, 0.0) + jnp.where(c2, 4.0, 0.0)) + (jnp.where(c3, 2.0, 0.0) + jnp.where(c4, 1.0, 0.0))


def _route(s1, s2):
    n = s1.shape[0] // SUBLANES
    x1 = [s1[SUBLANES * r:SUBLANES * (r + 1)] for r in range(n)]
    x2 = [s2[SUBLANES * r:SUBLANES * (r + 1)] for r in range(n)]
    v1 = _top16_sorted(x1)
    v2 = _top16_sorted(x2)
    sub = lax.broadcasted_iota(jnp.int32, v1[0].shape, 0)

    def spread(v, base):
        out = v[base + SUBLANES - 1]
        for k in range(SUBLANES - 2, -1, -1):
            out = jnp.where(sub == k, v[base + k], out)
        return out

    v2lo, v2hi, v1hi = spread(v2, 0), spread(v2, SUBLANES), spread(v1, SUBLANES)
    cand = [v1[0] + v2lo, v1[0] + v2hi] + [v1[i] + v2lo for i in range(1, SUBLANES)] + [v1hi + v2[0]]
    cand += [jnp.full_like(v1[0], -jnp.inf)] * (PEER_TOPK - len(cand))
    best = _top16_sorted(cand)
    tau, top = best[PEER_TOPK - 1], best[0]
    zsum = jnp.exp(best[0] - top)
    for k in range(1, PEER_TOPK):
        zsum = zsum + jnp.exp(best[k] - top)
    half_inv = 0.5 / zsum
    rank2, count, pa_half, pb = [], [], [], []
    for r in range(n):
        a, b = x1[r], x2[r]
        rk = _search16(lambda t: b < t, v2)
        rank2.append(jnp.where(b < v2[PEER_TOPK - 1], float(PEER_TOPK), rk))
        ct = _search16(lambda t: a + t >= tau, v2)
        count.append(jnp.where(a + v2[PEER_TOPK - 1] >= tau, float(PEER_TOPK), ct))
        pa_half.append(jnp.exp(a - v1[0]) * half_inv)
        pb.append(jnp.exp(b - v2[0]))
    cat = lambda xs: jnp.concatenate(xs, axis=0)
    return cat(rank2), cat(count), cat(pa_half), cat(pb)


GELU_C1 = math.sqrt(2.0 / math.pi)
GELU_C2 = GELU_C1 * 0.044715


def _gelu_times_two(a):
    t = jnp.tanh(a * (jnp.asarray(GELU_C1, a.dtype) + jnp.asarray(GELU_C2, a.dtype) * (a * a)))
    return a + a * t


def _peer_kernel(h2_ref, x1_ref, mod_ref, wpqT_ref, khi_ref, klo_ref, u_ref, vT_ref, o_ref,
                 rank_ref, pb_ref, cnt_ref, pa_ref, sc_ref, acc_ref):
    j = pl.program_id(1)
    tb = h2_ref.shape[0]
    subs = u_ref.shape[0] // PEER_SUB
    chunks = tb // LANES
    tiles = N_KEYS // BF16_ROWS
    h2 = h2_ref[...]

    @pl.when(j == 0)
    def _():
        acc_ref[...] = jnp.zeros_like(acc_ref)

        def scores(hd, slot):
            w_rows = wpqT_ref[pl.ds(pl.multiple_of(hd * 2 * PEER_HALF, 2 * PEER_HALF), 2 * PEER_HALF), :]
            qpT = lax.dot_general(w_rows, h2, NT_DIMS, preferred_element_type=F32)
            q_hi = qpT.astype(BF16)
            q_lo = (qpT - q_hi.astype(F32)).astype(BF16)
            for half in range(2):
                rows_q = slice(half * PEER_HALF, (half + 1) * PEER_HALF)
                k_hi, k_lo = khi_ref[hd, half], klo_ref[hd, half]
                sc = jnp.dot(k_hi, q_hi[rows_q], preferred_element_type=F32) + (
                    jnp.dot(k_hi, q_lo[rows_q], preferred_element_type=F32)
                    + jnp.dot(k_lo, q_hi[rows_q], preferred_element_type=F32))
                for c in range(chunks):
                    sc_ref[slot, half, c] = sc[:, c * LANES:(c + 1) * LANES]

        def tables(hd, slot):
            for c in range(chunks):
                rank2, count, pa_half, pb = _route(sc_ref[slot, 0, c], sc_ref[slot, 1, c])
                rank_ref[hd, c] = rank2.astype(BF16)
                pb_ref[hd, c] = pb.astype(BF16)
                cnt_ref[hd, c] = count
                pa_ref[hd, c] = pa_half

        def head_pair(hp, carry):
            hd = 2 * hp
            scores(hd + 1, 1)
            tables(hd, 0)
            scores(hd + 2, 0)
            tables(hd + 1, 1)
            return carry

        scores(0, 0)
        lax.fori_loop(0, PEER_HEADS // 2 - 1, head_pair, 0)
        scores(PEER_HEADS - 1, 1)
        tables(PEER_HEADS - 2, 0)
        tables(PEER_HEADS - 1, 1)

    rows = PEER_SUB // N_KEYS
    zero = jnp.zeros((), BF16)

    def first_matmul(sb):
        return lax.dot_general(u_ref[sb * PEER_SUB:(sb + 1) * PEER_SUB, :], h2, NT_DIMS,
                               preferred_element_type=F32)

    a_vals = {k: first_matmul(k) for k in range(min(PEER_LOOKAHEAD, subs))}
    for sb in range(subs):
        if sb + PEER_LOOKAHEAD < subs:
            a_vals[sb + PEER_LOOKAHEAD] = first_matmul(sb + PEER_LOOKAHEAD)
        aT = a_vals.pop(sb)
        e1 = [(j * subs + sb) * rows + r for r in range(rows)]
        w_cols = [[] for _ in range(rows)]
        for c in range(chunks):
            w = [jnp.zeros((N_KEYS, LANES), BF16) for _ in range(rows)]
            for hd in range(PEER_HEADS):
                rk, pb = rank_ref[hd, c], pb_ref[hd, c]
                for r in range(rows):
                    cnt = jnp.broadcast_to(cnt_ref[hd, c, pl.ds(e1[r], 1), :], (BF16_ROWS, LANES)).astype(BF16)
                    pa = jnp.broadcast_to(pa_ref[hd, c, pl.ds(e1[r], 1), :], (BF16_ROWS, LANES)).astype(BF16)
                    cnt = jnp.concatenate([cnt] * tiles, axis=0)
                    pa = jnp.concatenate([pa] * tiles, axis=0)
                    w[r] = w[r] + jnp.where(rk < cnt, pb, zero) * pa
            for r in range(rows):
                w_cols[r].append(w[r])
        zs = []
        for r in range(rows):
            g = _gelu_times_two(aT[r * N_KEYS:(r + 1) * N_KEYS].astype(BF16))
            zs.append(g * jnp.concatenate(w_cols[r], axis=1))
        zT = jnp.concatenate(zs, axis=0)
        acc_ref[...] += jnp.dot(vT_ref[:, sb * PEER_SUB:(sb + 1) * PEER_SUB], zT,
                                preferred_element_type=F32)

    @pl.when(j == pl.num_programs(1) - 1)
    def _():
        o_ref[...] = x1_ref[...] + mod_ref[0, 5:6, :] * acc_ref[...].T


def _peer(h2, x1, mod, w_pq, sub_keys, peer_u, peer_v, seq):
    t, d = h2.shape
    tb, eb = PEER_TOKENS, PEER_EXPERTS
    n_exp = peer_u.shape[0]
    assert eb % PEER_SUB == 0 and PEER_SUB % N_KEYS == 0
    chunks = tb // LANES
    packed = pltpu.VMEM((PEER_HEADS, chunks, N_KEYS, LANES), BF16)
    rowtab = pltpu.VMEM((PEER_HEADS, chunks, N_KEYS, LANES), F32)
    scores = pltpu.VMEM((2, 2, chunks, N_KEYS, LANES), F32)
    keys_hi = sub_keys.astype(BF16)
    keys_lo = (sub_keys.astype(F32) - keys_hi.astype(F32)).astype(BF16)
    return pl.pallas_call(
        _peer_kernel,
        grid=(t // tb, n_exp // eb),
        in_specs=[
            pl.BlockSpec((tb, d), lambda i, j: (i, 0)),
            pl.BlockSpec((tb, d), lambda i, j: (i, 0)),
            pl.BlockSpec((1, 6, d), lambda i, j: (i * tb // seq, 0, 0)),
            _full((w_pq.shape[1], d)),
            _full(sub_keys.shape),
            _full(sub_keys.shape),
            pl.BlockSpec((eb, d), lambda i, j: (j, 0)),
            pl.BlockSpec((d, eb), lambda i, j: (0, j)),
        ],
        out_specs=pl.BlockSpec((tb, d), lambda i, j: (i, 0)),
        out_shape=jax.ShapeDtypeStruct((t, d), F32),
        scratch_shapes=[packed, packed, rowtab, rowtab, scores, pltpu.VMEM((d, tb), F32)],
        compiler_params=_params("arbitrary", "arbitrary"),
        name="peer",
    )(h2, x1, mod, w_pq.T.astype(BF16), keys_hi, keys_lo, peer_u.astype(BF16), peer_v.T.astype(BF16))


def kernel(x, c, positions, w_ada, b_ada, norm1_g, w_in, mla_qa_g, mla_kva_g, w_uq, w_ukv, mla_q_g, mla_k_g,
           fox_q_g, fox_k_g, b_f, w_o, norm2_g, w_pq, sub_keys, peer_u, peer_v):
    b, s, d = x.shape
    assert s % PREP_TOKENS == 0 and s % ATTN_BLOCK == 0 and s % PROJ_TOKENS == 0 and s % PEER_TOKENS == 0
    assert PREP_TOKENS == ATTN_BLOCK and peer_u.shape[0] == N_KEYS * N_KEYS
    mod = _ada(c, w_ada, b_ada)
    qt, k, vt = _prep(x, positions, mod, norm1_g, w_in, mla_qa_g, mla_kva_g, w_uq, w_ukv, mla_q_g, mla_k_g,
                      fox_q_g, fox_k_g, b_f)
    ot = _attention(qt, k, vt)
    x1, h2 = _out_proj(ot, x, mod, w_o, norm2_g)
    out = _peer(h2.reshape(b * s, d), x1.reshape(b * s, d), mod, w_pq, sub_keys, peer_u, peer_v, s)
    return out.reshape(b, s, d).astype(x.dtype)
```

```python
import functools
import math

import jax
import jax.numpy as jnp
from jax import lax
from jax.experimental import pallas as pl
from jax.experimental.pallas import tpu as pltpu

F32 = jnp.float32
BF16 = jnp.bfloat16

EPS = 1e-6
LOG2E = 1.4426950408889634
ROPE_THETA = 10000.0

LANES = 128
SUBLANES = 8
VMEM_LIMIT_BYTES = 56 * 1024 * 1024

MLA_HEADS = 8
MLA_Q_RANK = 384
MLA_KV_RANK = 256
MLA_NOPE = 64
MLA_ROPE = 32
MLA_QK = MLA_NOPE + MLA_ROPE
MLA_V = 64
FOX_HEADS = 8
FOX_DIM = 64
FOX_WIDTH = FOX_HEADS * FOX_DIM
HEADS = MLA_HEADS + FOX_HEADS
HEAD_PAD = 128
AUG_ROWS = 16
V_ROWS = 80
PEER_HEADS = 8
N_KEYS = 128
PEER_HALF = 128
PEER_TOPK = 16
OFF_CQ = 0
OFF_CKV = OFF_CQ + MLA_Q_RANK
OFF_KPE = OFF_CKV + MLA_KV_RANK
OFF_FQ = OFF_KPE + MLA_ROPE
OFF_FK = OFF_FQ + FOX_WIDTH
OFF_FV = OFF_FK + FOX_WIDTH
OFF_FL = OFF_FV + FOX_WIDTH

PREP_TOKENS = 512
ATTN_BLOCK = 512
ATTN_HEADS = 2
PEER_TOKENS = 512
PEER_EXPERTS = 2048
PEER_SUB = 1024
PEER_LOOKAHEAD = 1
BF16_ROWS = 16

NT_DIMS = (((1,), (1,)), ((), ()))
TN_DIMS = (((0,), (0,)), ((), ()))


def _params(*semantics):
    return pltpu.CompilerParams(dimension_semantics=semantics, vmem_limit_bytes=VMEM_LIMIT_BYTES)


def _full(shape):
    n = len(shape)
    return pl.BlockSpec(shape, lambda *_: (0,) * n)


def _ada_kernel(c_ref, w_ref, b_ref, o_ref):
    c = c_ref[...]
    s = c / (1.0 + jnp.exp(-c))
    o_ref[...] = jnp.dot(s, w_ref[...], preferred_element_type=F32,
                         precision=lax.Precision.HIGHEST) + b_ref[...]


def _ada(c, w_ada, b_ada):
    b, d = c.shape
    n = w_ada.shape[1]
    rows = 8
    cp = jnp.zeros((rows, d), F32).at[:b].set(c.astype(F32))
    out = pl.pallas_call(
        _ada_kernel,
        grid=(n // d,),
        in_specs=[_full((rows, d)), pl.BlockSpec((d, d), lambda j: (0, j)), pl.BlockSpec((1, d), lambda j: (0, j))],
        out_specs=pl.BlockSpec((rows, d), lambda j: (0, j)),
        out_shape=jax.ShapeDtypeStruct((rows, n), F32),
        compiler_params=_params("arbitrary"),
        name="adaln",
    )(cp, w_ada.astype(F32), b_ada.astype(F32).reshape(1, n))
    return out[:b].reshape(b, n // d, d)


def _split3(f):
    hi = f.astype(BF16).astype(F32)
    r = f - hi
    mid = r.astype(BF16).astype(F32)
    return hi, mid, r - mid


def _aug_block(vals, ones_first, width):
    row = lax.broadcasted_iota(jnp.int32, (AUG_ROWS, width), 0)
    v0, v1, v2 = vals
    off = 3 if ones_first else 0
    blk = jnp.where(row == off, v0, jnp.where(row == off + 1, v1, jnp.where(row == off + 2, v2, 0.0)))
    one_lo = 0 if ones_first else 3
    return jnp.where((row >= one_lo) & (row < one_lo + 3), 1.0, blk)


def _prep_kernel(x_ref, pos_ref, mod_ref, g1_ref, winT_ref, gqa_ref, gkva_ref, wuqT_ref, wukvT_ref,
                 gq_ref, gk_ref, gfq_ref, gfk_ref, bf_ref, invf_ref,
                 qt_ref, k_ref, vt_ref, carry_ref):
    tm = x_ref.shape[1]

    @pl.when(pl.program_id(1) == 0)
    def _():
        carry_ref[...] = jnp.zeros_like(carry_ref)

    x = x_ref[0]
    y = x * lax.rsqrt(jnp.mean(x * x, axis=-1, keepdims=True) + EPS) * g1_ref[...]
    h = y * (1.0 + mod_ref[0, 1:2, :]) + mod_ref[0, 0:1, :]
    projT = lax.dot_general(winT_ref[...], h.astype(BF16), NT_DIMS, preferred_element_type=F32)

    def rms_rows(v, n):
        return lax.rsqrt(jnp.sum(v * v, axis=0, keepdims=True) * (1.0 / n) + EPS)

    cq = projT[OFF_CQ:OFF_CKV]
    cqn = (cq * rms_rows(cq, MLA_Q_RANK) * gqa_ref[...]).astype(BF16)
    qaT = jnp.dot(wuqT_ref[...], cqn, preferred_element_type=F32)
    ckv = projT[OFF_CKV:OFF_KPE]
    ckvn = (ckv * rms_rows(ckv, MLA_KV_RANK) * gkva_ref[...]).astype(BF16)
    kvT = jnp.dot(wukvT_ref[...], ckvn, preferred_element_type=F32)
    kpe = projT[OFF_KPE:OFF_FQ]

    ang = pos_ref[0].astype(F32) * invf_ref[...]
    cos, sin = jnp.cos(ang), jnp.sin(ang)
    half = MLA_ROPE // 2

    def rope(v):
        v1, v2 = v[:half], v[half:]
        return v1 * cos - v2 * sin, v2 * cos + v1 * sin

    gq, gk = gq_ref[...], gk_ref[...]
    kpe_ss = jnp.sum(kpe * kpe, axis=0, keepdims=True)
    kr1, kr2 = rope(kpe * gk[MLA_NOPE:])
    zpad_mla = jnp.zeros((HEAD_PAD - MLA_QK, tm), F32)
    ones_rows = jnp.ones((V_ROWS - MLA_V, tm), BF16)
    q_scale = MLA_QK ** -0.5 * LOG2E
    for hd in range(MLA_HEADS):
        q = qaT[hd * MLA_QK:(hd + 1) * MLA_QK]
        qn = q * rms_rows(q, MLA_QK) * gq
        r1, r2 = rope(qn[MLA_NOPE:])
        qt_ref[0, hd] = (jnp.concatenate([qn[:MLA_NOPE], r1, r2, zpad_mla], axis=0) * q_scale).astype(BF16)
        base = hd * (MLA_NOPE + MLA_V)
        kn = kvT[base:base + MLA_NOPE]
        r = lax.rsqrt((jnp.sum(kn * kn, axis=0, keepdims=True) + kpe_ss) * (1.0 / MLA_QK) + EPS)
        kblk = jnp.concatenate([kn * r * gk[:MLA_NOPE], kr1 * r, kr2 * r, zpad_mla], axis=0)
        k_ref[0, hd] = kblk.T.astype(BF16)
        vt_ref[0, hd, 0, :MLA_V] = kvT[base + MLA_NOPE:base + MLA_NOPE + MLA_V].astype(BF16)
        vt_ref[0, hd, 0, MLA_V:] = ones_rows

    z = projT[OFF_FL:OFF_FL + FOX_HEADS] + bf_ref[...]
    logf = -(jnp.maximum(-z, 0.0) + jnp.log1p(jnp.exp(-jnp.abs(z))))
    lhi, lmid, llo = _split3(logf)
    tri = (lax.broadcasted_iota(jnp.int32, (tm, tm), 0) <= lax.broadcasted_iota(jnp.int32, (tm, tm), 1))
    tri = jnp.where(tri, 1.0, 0.0).astype(BF16)
    parts = jnp.dot(jnp.concatenate([lhi, lmid, llo], axis=0).astype(BF16), tri, preferred_element_type=F32)
    fcum = (parts[2 * FOX_HEADS:] + parts[FOX_HEADS:2 * FOX_HEADS]) + parts[:FOX_HEADS] + carry_ref[:, 0:1]
    carry_ref[...] = jnp.broadcast_to(fcum[:, tm - 1:tm], carry_ref.shape)
    fc = fcum * LOG2E

    gfq, gfk = gfq_ref[...], gfk_ref[...]
    zpad_fox = jnp.zeros((HEAD_PAD - FOX_DIM - AUG_ROWS, tm), F32)
    f_scale = FOX_DIM ** -0.5 * LOG2E
    for hd in range(FOX_HEADS):
        fq = projT[OFF_FQ + hd * FOX_DIM:OFF_FQ + (hd + 1) * FOX_DIM]
        fk = projT[OFF_FK + hd * FOX_DIM:OFF_FK + (hd + 1) * FOX_DIM]
        fv = projT[OFF_FV + hd * FOX_DIM:OFF_FV + (hd + 1) * FOX_DIM]
        f_row = fc[hd:hd + 1]
        qn = fq * rms_rows(fq, FOX_DIM) * gfq * f_scale
        qblk = jnp.concatenate([qn, _aug_block(_split3(f_row), False, tm), zpad_fox], axis=0)
        qt_ref[0, MLA_HEADS + hd] = qblk.astype(BF16)
        kn = fk * rms_rows(fk, FOX_DIM) * gfk
        kblk = jnp.concatenate([kn, _aug_block(_split3(-f_row), True, tm), zpad_fox], axis=0)
        k_ref[0, MLA_HEADS + hd] = kblk.T.astype(BF16)
        vt_ref[0, MLA_HEADS + hd, 0, :MLA_V] = fv.astype(BF16)
        vt_ref[0, MLA_HEADS + hd, 0, MLA_V:] = ones_rows


def _prep(x, positions, mod, norm1_g, w_in, mla_qa_g, mla_kva_g, w_uq, w_ukv, mla_q_g, mla_k_g,
          fox_q_g, fox_k_g, b_f):
    b, s, d = x.shape
    tm = PREP_TOKENS
    nj = s // tm
    col = lambda v: v.astype(F32).reshape(-1, 1)
    inv_freq = ROPE_THETA ** (-jnp.arange(0, MLA_ROPE, 2, dtype=F32) / MLA_ROPE)
    in_width = w_in.shape[1]
    args = (x, positions.reshape(b, 1, s), mod, norm1_g.astype(F32).reshape(1, d), w_in.T.astype(BF16),
            col(mla_qa_g), col(mla_kva_g), w_uq.T.astype(BF16), w_ukv.T.astype(BF16),
            col(mla_q_g), col(mla_k_g), col(fox_q_g), col(fox_k_g), col(b_f), col(inv_freq))
    in_specs = [
        pl.BlockSpec((1, tm, d), lambda i, j: (i, j, 0)),
        pl.BlockSpec((1, 1, tm), lambda i, j: (i, 0, j)),
        pl.BlockSpec((1, 6, d), lambda i, j: (i, 0, 0)),
    ] + [_full(a.shape) for a in args[3:]]
    assert in_width == OFF_FL + FOX_HEADS
    return pl.pallas_call(
        _prep_kernel,
        grid=(b, nj),
        in_specs=in_specs,
        out_specs=[
            pl.BlockSpec((1, HEADS, HEAD_PAD, tm), lambda i, j: (i, 0, 0, j)),
            pl.BlockSpec((1, HEADS, tm, HEAD_PAD), lambda i, j: (i, 0, j, 0)),
            pl.BlockSpec((1, HEADS, 1, V_ROWS, tm), lambda i, j: (i, 0, j, 0, 0)),
        ],
        out_shape=[
            jax.ShapeDtypeStruct((b, HEADS, HEAD_PAD, s), BF16),
            jax.ShapeDtypeStruct((b, HEADS, s, HEAD_PAD), BF16),
            jax.ShapeDtypeStruct((b, HEADS, nj, V_ROWS, tm), BF16),
        ],
        scratch_shapes=[pltpu.VMEM((FOX_HEADS, LANES), F32)],
        compiler_params=_params("arbitrary", "arbitrary"),
        name="token_prep",
    )(*args)


def _attn_kernel(qt_ref, k_ref, vt_ref, o_ref, s_ref):
    nh = qt_ref.shape[1]
    blk = qt_ref.shape[3]
    i = pl.program_id(2)
    qt = [qt_ref[0, h] for h in range(nh)]

    def scores(slot, j):
        for h in range(nh):
            kb = k_ref[0, h, pl.ds(pl.multiple_of(j * blk, blk), blk), :]
            s_ref[h, slot] = jnp.dot(kb, qt[h], preferred_element_type=F32)

    def consume(slot, j, carry, diagonal):
        out = []
        for h in range(nh):
            m, acc = carry[h]
            s = s_ref[h, slot]
            if diagonal:
                keep = lax.broadcasted_iota(jnp.int32, (blk, blk), 0) <= lax.broadcasted_iota(jnp.int32, (blk, blk), 1)
                s = jnp.where(keep, s, -jnp.inf)
            m_new = jnp.maximum(m, jnp.max(s, axis=0, keepdims=True))
            p = jnp.exp2(s - m_new).astype(BF16)
            acc = jnp.exp2(m - m_new) * acc + jnp.dot(vt_ref[0, h, j], p, preferred_element_type=F32)
            out.append((m_new, acc))
        return tuple(out)

    def finish(carry):
        for h in range(nh):
            _, acc = carry[h]
            o_ref[0, h * MLA_V:(h + 1) * MLA_V, :] = (acc[:MLA_V] / acc[MLA_V:MLA_V + 1]).astype(o_ref.dtype)

    def pair(jj, carry):
        j0 = 2 * jj
        scores(1, j0 + 1)
        carry = consume(0, j0, carry, False)
        scores(0, j0 + 2)
        return consume(1, j0 + 1, carry, False)

    init = tuple((jnp.full((1, blk), -jnp.inf, F32), jnp.zeros((vt_ref.shape[3], blk), F32)) for _ in range(nh))
    scores(0, 0)
    carry = lax.fori_loop(0, i // 2, pair, init)

    @pl.when(i % 2 == 0)
    def _():
        finish(consume(0, i, carry, True))

    @pl.when(i % 2 == 1)
    def _():
        scores(1, i)
        finish(consume(1, i, consume(0, i - 1, carry, False), True))


def _attention(qt, k, vt):
    b, heads, _, s = qt.shape
    blk, nh = ATTN_BLOCK, ATTN_HEADS
    assert vt.shape[4] == blk and heads % nh == 0
    return pl.pallas_call(
        _attn_kernel,
        grid=(b, heads // nh, s // blk),
        in_specs=[
            pl.BlockSpec((1, nh, HEAD_PAD, blk), lambda bi, h, i: (bi, h, 0, i)),
            pl.BlockSpec((1, nh, s, HEAD_PAD), lambda bi, h, i: (bi, h, 0, 0)),
            pl.BlockSpec((1, nh, s // blk, V_ROWS, blk), lambda bi, h, i: (bi, h, 0, 0, 0)),
        ],
        out_specs=pl.BlockSpec((1, nh * MLA_V, blk), lambda bi, h, i: (bi, h, i)),
        out_shape=jax.ShapeDtypeStruct((b, heads * MLA_V, s), BF16),
        scratch_shapes=[pltpu.VMEM((nh, 2, blk, blk), F32)],
        compiler_params=_params("arbitrary", "arbitrary", "arbitrary"),
        name="flash_attention",
    )(qt, k, vt)


def _batcher_pairs(n):
    pairs = []
    p = 1
    while p < n:
        k = p
        while k >= 1:
            for j in range(k % p, n - k, 2 * k):
                for i in range(min(k, n - j - k)):
                    if (i + j) // (2 * p) == (i + j + k) // (2 * p):
                        pairs.append((i + j, i + j + k))
            k //= 2
        p *= 2
    return pairs


SORT16 = _batcher_pairs(PEER_TOPK)
BITONIC16 = [(i, i + d) for d in (8, 4, 2, 1) for i in range(PEER_TOPK) if (i // d) % 2 == 0]


def _exchange(x, pairs):
    for i, j in pairs:
        x[i], x[j] = jnp.maximum(x[i], x[j]), jnp.minimum(x[i], x[j])
    return x


def _top16_sorted(x):
    x = _exchange(list(x), SORT16)
    for shift in (4, 2, 1):
        y = [jnp.maximum(x[k], pltpu.roll(x[PEER_TOPK - 1 - k], shift, 0)) for k in range(PEER_TOPK)]
        x = _exchange(y, BITONIC16)
    return x


def _search16(test, vb):
    c1 = test(vb[7])
    c2 = test(jnp.where(c1, vb[11], vb[3]))
    c3 = test(jnp.where(c1, jnp.where(c2, vb[13], vb[9]), jnp.where(c2, vb[5], vb[1])))
    c4 = test(jnp.where(c1,
                        jnp.where(c2, jnp.where(c3, vb[14], vb[12]), jnp.where(c3, vb[10], vb[8])),
                        jnp.where(c2, jnp.where(c3, vb[6], vb[4]), jnp.where(c3, vb[2], vb[0]))))
    return (jnp.where(c1, 8.0, 0.0) + jnp.where(c2, 4.0, 0.0)) + (jnp.where(c3, 2.0, 0.0) + jnp.where(c4, 1.0, 0.0))


def _route(s1, s2):
    n = s1.shape[0] // SUBLANES
    x1 = [s1[SUBLANES * r:SUBLANES * (r + 1)] for r in range(n)]
    x2 = [s2[SUBLANES * r:SUBLANES * (r + 1)] for r in range(n)]
    v1 = _top16_sorted(x1)
    v2 = _top16_sorted(x2)
    sub = lax.broadcasted_iota(jnp.int32, v1[0].shape, 0)

    def spread(v, base):
        out = v[base + SUBLANES - 1]
        for k in range(SUBLANES - 2, -1, -1):
            out = jnp.where(sub == k, v[base + k], out)
        return out

    v2lo, v2hi, v1hi = spread(v2, 0), spread(v2, SUBLANES), spread(v1, SUBLANES)
    cand = [v1[0] + v2lo, v1[0] + v2hi] + [v1[i] + v2lo for i in range(1, SUBLANES)] + [v1hi + v2[0]]
    cand += [jnp.full_like(v1[0], -jnp.inf)] * (PEER_TOPK - len(cand))
    best = _top16_sorted(cand)
    tau, top = best[PEER_TOPK - 1], best[0]
    zsum = jnp.exp(best[0] - top)
    for k in range(1, PEER_TOPK):
        zsum = zsum + jnp.exp(best[k] - top)
    half_inv = 0.5 / zsum
    rank2, count, pa_half, pb = [], [], [], []
    for r in range(n):
        a, b = x1[r], x2[r]
        rk = _search16(lambda t: b < t, v2)
        rank2.append(jnp.where(b < v2[PEER_TOPK - 1], float(PEER_TOPK), rk))
        ct = _search16(lambda t: a + t >= tau, v2)
        count.append(jnp.where(a + v2[PEER_TOPK - 1] >= tau, float(PEER_TOPK), ct))
        pa_half.append(jnp.exp(a - v1[0]) * half_inv)
        pb.append(jnp.exp(b - v2[0]))
    cat = lambda xs: jnp.concatenate(xs, axis=0)
    return cat(rank2), cat(count), cat(pa_half), cat(pb)


GELU_C1 = math.sqrt(2.0 / math.pi)
GELU_C2 = GELU_C1 * 0.044715


def _gelu_times_two(a):
    t = jnp.tanh(a * (jnp.asarray(GELU_C1, a.dtype) + jnp.asarray(GELU_C2, a.dtype) * (a * a)))
    return a + a * t


def _peer_kernel(ot_ref, x_ref, mod_ref, wo_ref, g2_ref, wpqT_ref, khi_ref, klo_ref, u_ref, v_ref, o_ref,
                 rank_ref, pb_ref, cnt_ref, pa_ref, sc_ref, acc_ref, h2_ref, x1_ref):
    j = pl.program_id(1)
    tb = x_ref.shape[0]
    subs = u_ref.shape[0] // PEER_SUB
    chunks = tb // LANES
    tiles = N_KEYS // BF16_ROWS

    @pl.when(j == 0)
    def _():
        acc_ref[...] = jnp.zeros_like(acc_ref)
        attn = lax.dot_general(ot_ref[0], wo_ref[...], TN_DIMS, preferred_element_type=F32)
        x1 = x_ref[...] + mod_ref[0, 2:3, :] * attn
        x1_ref[...] = x1
        y = x1 * lax.rsqrt(jnp.mean(x1 * x1, axis=-1, keepdims=True) + EPS) * g2_ref[...]
        h2_ref[...] = (y * (1.0 + mod_ref[0, 4:5, :]) + mod_ref[0, 3:4, :]).astype(h2_ref.dtype)
        h2 = h2_ref[...]

        def scores(hd, slot):
            w_rows = wpqT_ref[pl.ds(pl.multiple_of(hd * 2 * PEER_HALF, 2 * PEER_HALF), 2 * PEER_HALF), :]
            qpT = lax.dot_general(w_rows, h2, NT_DIMS, preferred_element_type=F32)
            q_hi = qpT.astype(BF16)
            q_lo = (qpT - q_hi.astype(F32)).astype(BF16)
            for half in range(2):
                rows_q = slice(half * PEER_HALF, (half + 1) * PEER_HALF)
                k_hi, k_lo = khi_ref[hd, half], klo_ref[hd, half]
                sc = jnp.dot(k_hi, q_hi[rows_q], preferred_element_type=F32) + (
                    jnp.dot(k_hi, q_lo[rows_q], preferred_element_type=F32)
                    + jnp.dot(k_lo, q_hi[rows_q], preferred_element_type=F32))
                for c in range(chunks):
                    sc_ref[slot, half, c] = sc[:, c * LANES:(c + 1) * LANES]

        def tables(hd, slot):
            for c in range(chunks):
                rank2, count, pa_half, pb = _route(sc_ref[slot, 0, c], sc_ref[slot, 1, c])
                rank_ref[hd, c] = rank2.astype(BF16)
                pb_ref[hd, c] = pb.astype(BF16)
                cnt_ref[hd, c] = count
                pa_ref[hd, c] = pa_half

        def head_pair(hp, carry):
            hd = 2 * hp
            scores(hd + 1, 1)
            tables(hd, 0)
            scores(hd + 2, 0)
            tables(hd + 1, 1)
            return carry

        scores(0, 0)
        lax.fori_loop(0, PEER_HEADS // 2 - 1, head_pair, 0)
        scores(PEER_HEADS - 1, 1)
        tables(PEER_HEADS - 2, 0)
        tables(PEER_HEADS - 1, 1)

    h2 = h2_ref[...]
    rows = PEER_SUB // N_KEYS
    zero = jnp.zeros((), BF16)

    def first_matmul(sb):
        return lax.dot_general(u_ref[sb * PEER_SUB:(sb + 1) * PEER_SUB, :], h2, NT_DIMS,
                               preferred_element_type=F32)

    a_vals = {k: first_matmul(k) for k in range(min(PEER_LOOKAHEAD, subs))}
    for sb in range(subs):
        if sb + PEER_LOOKAHEAD < subs:
            a_vals[sb + PEER_LOOKAHEAD] = first_matmul(sb + PEER_LOOKAHEAD)
        aT = a_vals.pop(sb)
        e1 = [(j * subs + sb) * rows + r for r in range(rows)]
        w_cols = [[] for _ in range(rows)]
        for c in range(chunks):
            w = [jnp.zeros((N_KEYS, LANES), BF16) for _ in range(rows)]
            for hd in range(PEER_HEADS):
                rk, pb = rank_ref[hd, c], pb_ref[hd, c]
                for r in range(rows):
                    cnt = jnp.broadcast_to(cnt_ref[hd, c, pl.ds(e1[r], 1), :], (BF16_ROWS, LANES)).astype(BF16)
                    pa = jnp.broadcast_to(pa_ref[hd, c, pl.ds(e1[r], 1), :], (BF16_ROWS, LANES)).astype(BF16)
                    cnt = jnp.concatenate([cnt] * tiles, axis=0)
                    pa = jnp.concatenate([pa] * tiles, axis=0)
                    w[r] = w[r] + jnp.where(rk < cnt, pb, zero) * pa
            for r in range(rows):
                w_cols[r].append(w[r])
        zs = []
        for r in range(rows):
            g = _gelu_times_two(aT[r * N_KEYS:(r + 1) * N_KEYS].astype(BF16))
            zs.append(g * jnp.concatenate(w_cols[r], axis=1))
        zT = jnp.concatenate(zs, axis=0)
        acc_ref[...] += lax.dot_general(v_ref[sb * PEER_SUB:(sb + 1) * PEER_SUB, :], zT, TN_DIMS,
                                        preferred_element_type=F32)

    @pl.when(j == pl.num_programs(1) - 1)
    def _():
        o_ref[...] = x1_ref[...] + mod_ref[0, 5:6, :] * acc_ref[...].T


def _peer(ot, x, mod, w_o, norm2_g, w_pq, sub_keys, peer_u, peer_v, seq):
    t, d = x.shape
    width = ot.shape[1]
    tb, eb = PEER_TOKENS, PEER_EXPERTS
    n_exp = peer_u.shape[0]
    assert eb % PEER_SUB == 0 and PEER_SUB % N_KEYS == 0
    chunks = tb // LANES
    packed = pltpu.VMEM((PEER_HEADS, chunks, N_KEYS, LANES), BF16)
    rowtab = pltpu.VMEM((PEER_HEADS, chunks, N_KEYS, LANES), F32)
    scores = pltpu.VMEM((2, 2, chunks, N_KEYS, LANES), F32)
    keys_hi = sub_keys.astype(BF16)
    keys_lo = (sub_keys.astype(F32) - keys_hi.astype(F32)).astype(BF16)
    return pl.pallas_call(
        _peer_kernel,
        grid=(t // tb, n_exp // eb),
        in_specs=[
            pl.BlockSpec((1, width, tb), lambda i, j: (i * tb // seq, 0, i % (seq // tb))),
            pl.BlockSpec((tb, d), lambda i, j: (i, 0)),
            pl.BlockSpec((1, 6, d), lambda i, j: (i * tb // seq, 0, 0)),
            _full((width, d)),
            _full((1, d)),
            _full((w_pq.shape[1], d)),
            _full(sub_keys.shape),
            _full(sub_keys.shape),
            pl.BlockSpec((eb, d), lambda i, j: (j, 0)),
            pl.BlockSpec((eb, d), lambda i, j: (j, 0)),
        ],
        out_specs=pl.BlockSpec((tb, d), lambda i, j: (i, 0)),
        out_shape=jax.ShapeDtypeStruct((t, d), F32),
        scratch_shapes=[packed, packed, rowtab, rowtab, scores, pltpu.VMEM((d, tb), F32),
                        pltpu.VMEM((tb, d), BF16), pltpu.VMEM((tb, d), F32)],
        compiler_params=_params("arbitrary", "arbitrary"),
        name="peer",
    )(ot, x, mod, w_o.astype(BF16), norm2_g.astype(F32).reshape(1, d), w_pq.T.astype(BF16), keys_hi, keys_lo,
      peer_u.astype(BF16), peer_v.astype(BF16))


def kernel(x, c, positions, w_ada, b_ada, norm1_g, w_in, mla_qa_g, mla_kva_g, w_uq, w_ukv, mla_q_g, mla_k_g,
           fox_q_g, fox_k_g, b_f, w_o, norm2_g, w_pq, sub_keys, peer_u, peer_v):
    b, s, d = x.shape
    assert s % PREP_TOKENS == 0 and s % ATTN_BLOCK == 0 and s % PEER_TOKENS == 0
    assert PREP_TOKENS == ATTN_BLOCK and peer_u.shape[0] == N_KEYS * N_KEYS
    mod = _ada(c, w_ada, b_ada)
    qt, k, vt = _prep(x, positions, mod, norm1_g, w_in, mla_qa_g, mla_kva_g, w_uq, w_ukv, mla_q_g, mla_k_g,
                      fox_q_g, fox_k_g, b_f)
    ot = _attention(qt, k, vt)
    out = _peer(ot, x.reshape(b * s, d), mod, w_o, norm2_g, w_pq, sub_keys, peer_u, peer_v, s)
    return out.reshape(b, s, d).astype(x.dtype)
```
